```python
import jax, jax.numpy as jnp
from jax import lax
import numpy as np

D_MODEL = 1024
BATCH = 4
SEQ = 8192
DEPTH = 2

HGRN_HEADS = 8
HGRN_EXPAND = 128
HGRN_VDIM = 128
HGRN_KEY_WIDTH = HGRN_HEADS * HGRN_EXPAND
HGRN_VAL_WIDTH = HGRN_HEADS * HGRN_VDIM
HGRN_SCALE = HGRN_EXPAND ** -0.5
CHUNK = 64
F_MIN = 1e-6
CONV_CH = D_MODEL
CONV_K = 3
D_FF = 2816
N_EXPERTS = 8
TOP_K = 2
EPS = 1e-6
N_DENSE = (DEPTH + 1) // 2
N_MOE = DEPTH // 2
IN_SPLITS = (HGRN_KEY_WIDTH, HGRN_KEY_WIDTH, HGRN_VAL_WIDTH, HGRN_VAL_WIDTH,
             CONV_CH, CONV_CH, CONV_CH, D_MODEL, D_MODEL)
IN_WIDTH = sum(IN_SPLITS)

kernel_name = "hgrn2_shortconv_gated_merge_moe"


def rms_norm(x, w):
    xf = x.astype(jnp.float32)
    y = xf * lax.rsqrt(jnp.mean(xf * xf, axis=-1, keepdims=True) + EPS)
    return (y * w.astype(jnp.float32)).astype(x.dtype)


def split_projection(h, w):
    outs = []
    start = 0
    for width in IN_SPLITS:
        outs.append(jnp.einsum('bsd,de->bse', h, w[:, start:start + width]))
        start += width
    return outs


def hgrn2_chunk_recurrence(q, k, v, log_f):
    bsz, seq, nh, kd = q.shape
    vd = v.shape[-1]
    n_chunks = seq // CHUNK

    def to_chunks(t):
        return t.reshape(bsz, n_chunks, CHUNK, nh, t.shape[-1]).transpose(1, 0, 3, 2, 4)

    causal = jnp.tril(jnp.ones((CHUNK, CHUNK), dtype=bool))[:, :, None]

    def step(state, inp):
        qc, kc, vc, gc = inp
        cum = jnp.cumsum(gc, axis=2)
        diff = cum[:, :, :, None, :] - cum[:, :, None, :, :]
        decay = jnp.where(causal, jnp.exp(jnp.where(causal, diff, 0.0)), 0.0)
        scores = jnp.einsum('bhtk,bhtsk,bhsk->bhts', qc, decay, kc)
        o = (jnp.einsum('bhts,bhsv->bhtv', scores, vc)
             + jnp.einsum('bhtk,bhkv->bhtv', qc * jnp.exp(cum), state))
        last = cum[:, :, -1:, :]
        state = (jnp.exp(last[:, :, 0, :])[..., None] * state
                 + jnp.einsum('bhsk,bhsv->bhkv', kc * jnp.exp(last - cum), vc))
        return state, o

    s0 = jnp.zeros((bsz, nh, kd, vd), jnp.float32)
    _, o = lax.scan(step, s0, (to_chunks(q), to_chunks(k), to_chunks(v), to_chunks(log_f)))
    return o.transpose(1, 0, 3, 2, 4).reshape(bsz, seq, nh, vd)


def hybrid_mixer(h, w_in, lb, g_norm_w, conv_w, w_proj_hgrn, w_proj_conv, w_out):
    bsz, seq, _ = h.shape
    q, f_logit, i_in, g_out, b_gate, c_gate, u, gate_a, gate_b = split_projection(h, w_in)

    def heads(t):
        return t.reshape(bsz, seq, HGRN_HEADS, -1).astype(jnp.float32)

    z = heads(f_logit)
    lbh = lb.astype(jnp.float32).reshape(HGRN_HEADS, HGRN_EXPAND)
    f = lbh + (1.0 - lbh) * jax.nn.sigmoid(z)
    log_f = jnp.log(jnp.maximum(f, F_MIN))
    k = (1.0 - lbh) * jax.nn.sigmoid(-z)
    qh = jax.nn.silu(heads(q)) * HGRN_SCALE
    o = hgrn2_chunk_recurrence(qh, k, heads(i_in), log_f)
    o = rms_norm(o, g_norm_w) * jax.nn.silu(heads(g_out))
    y_a = jnp.einsum('bse,ed->bsd', o.reshape(bsz, seq, HGRN_VAL_WIDTH).astype(h.dtype), w_proj_hgrn)

    v = b_gate * u
    vp = jnp.pad(v, ((0, 0), (CONV_K - 1, 0), (0, 0)))
    conv = conv_w[0] * vp[:, 0:seq, :]
    for j in range(1, CONV_K):
        conv = conv + conv_w[j] * vp[:, j:j + seq, :]
    y_b = jnp.einsum('bse,ed->bsd', c_gate * conv, w_proj_conv)

    merged = jax.nn.sigmoid(gate_a) * y_a + jax.nn.sigmoid(gate_b) * y_b
    return jnp.einsum('bsd,de->bse', merged, w_out)


def swiglu(h, w1, w3, w2):
    a = jnp.einsum('bsd,df->bsf', h, w1)
    b = jnp.einsum('bsd,df->bsf', h, w3)
    return jnp.einsum('bsf,fd->bsd', jax.nn.silu(a) * b, w2)


def moe_swiglu(h, router_w, w1, w3, w2):
    bsz, seq, d = h.shape
    t = h.reshape(bsz * seq, d)
    logits = jnp.einsum('td,de->te', t, router_w).astype(jnp.float32)
    top_vals, top_idx = lax.top_k(logits, TOP_K)
    top_w = jax.nn.softmax(top_vals, axis=-1)
    gates = jnp.sum(jax.nn.one_hot(top_idx, N_EXPERTS, dtype=jnp.float32) * top_w[..., None], axis=1)
    gates = gates.astype(h.dtype)
    out = jnp.zeros_like(t)
    for e in range(N_EXPERTS):
        a = t @ w1[e]
        b = t @ w3[e]
        out = out + gates[:, e:e + 1] * ((jax.nn.silu(a) * b) @ w2[e])
    return out.reshape(bsz, seq, d)


def setup_inputs(seed: int = 0) -> dict:
    key = jax.random.key(seed)
    ks = jax.random.split(key, 20)
    nrm = lambda k, shape, fan_in: jax.random.normal(k, shape, jnp.float32) * (fan_in ** -0.5)
    gain = lambda k, shape: 1.0 + 0.05 * jax.random.normal(k, shape, jnp.float32)
    return {
        "x": jax.random.normal(ks[0], (BATCH, SEQ, D_MODEL), jnp.float32),
        "w_in": nrm(ks[1], (DEPTH, D_MODEL, IN_WIDTH), D_MODEL),
        "lower_bounds": 0.1 * jax.random.normal(ks[2], (DEPTH, HGRN_KEY_WIDTH), jnp.float32),
        "hgrn_norm_w": gain(ks[3], (DEPTH, HGRN_VDIM)),
        "conv_w": nrm(ks[4], (DEPTH, CONV_K, CONV_CH), CONV_K),
        "w_proj_hgrn": nrm(ks[5], (DEPTH, HGRN_VAL_WIDTH, D_MODEL), HGRN_VAL_WIDTH),
        "w_proj_conv": nrm(ks[6], (DEPTH, CONV_CH, D_MODEL), CONV_CH),
        "w_out": nrm(ks[7], (DEPTH, D_MODEL, D_MODEL), D_MODEL),
        "norm_mix": gain(ks[8], (DEPTH, D_MODEL)),
        "norm_ffn": gain(ks[9], (DEPTH, D_MODEL)),
        "dense_w1": nrm(ks[10], (N_DENSE, D_MODEL, D_FF), D_MODEL),
        "dense_w3": nrm(ks[11], (N_DENSE, D_MODEL, D_FF), D_MODEL),
        "dense_w2": nrm(ks[12], (N_DENSE, D_FF, D_MODEL), D_FF),
        "router_w": nrm(ks[13], (N_MOE, D_MODEL, N_EXPERTS), D_MODEL),
        "expert_w1": nrm(ks[14], (N_MOE, N_EXPERTS, D_MODEL, D_FF), D_MODEL),
        "expert_w3": nrm(ks[15], (N_MOE, N_EXPERTS, D_MODEL, D_FF), D_MODEL),
        "expert_w2": nrm(ks[16], (N_MOE, N_EXPERTS, D_FF, D_MODEL), D_FF),
        "final_norm": gain(ks[17], (D_MODEL,)),
    }


def reference(x, w_in, lower_bounds, hgrn_norm_w, conv_w, w_proj_hgrn, w_proj_conv, w_out,
              norm_mix, norm_ffn, dense_w1, dense_w3, dense_w2, router_w,
              expert_w1, expert_w3, expert_w2, final_norm):
    lb_soft = jax.nn.softmax(lower_bounds.astype(jnp.float32), axis=0)
    lb_all = jnp.cumsum(lb_soft, axis=0) - lb_soft[0]
    for layer in range(DEPTH):
        h = rms_norm(x, norm_mix[layer])
        x = x + hybrid_mixer(h, w_in[layer], lb_all[layer], hgrn_norm_w[layer], conv_w[layer],
                             w_proj_hgrn[layer], w_proj_conv[layer], w_out[layer])
        h = rms_norm(x, norm_ffn[layer])
        if layer % 2 == 0:
            j = layer // 2
            x = x + swiglu(h, dense_w1[j], dense_w3[j], dense_w2[j])
        else:
            j = layer // 2
            x = x + moe_swiglu(h, router_w[j], expert_w1[j], expert_w3[j], expert_w2[j])
    return rms_norm(x, final_norm)
```

```python
import functools

import jax
import jax.numpy as jnp
from jax import lax
from jax.experimental import pallas as pl
from jax.experimental.pallas import tpu as pltpu

D_MODEL = 1024
HEADS = 8
HEAD_DIM = 128
HGRN_SCALE = HEAD_DIM ** -0.5
F_MIN = 1e-6
CONV_K = 3
D_FF = 2816
N_EXPERTS = 8
EPS = 1e-6
N_GROUPS = 9
G_Q, G_F, G_I, G_G, G_B, G_C, G_U, G_GA, G_GB = range(N_GROUPS)

LANES = 128
BF16_SUBLANES = 16
VMEM_LIMIT = 56 * 1024 * 1024

TM_PROJ = 1024
TM_MIX = 512
TM_FFN = 512
FF_CHUNK = D_FF // 2
CT_REC = 1024
CHUNK = 128
SUB = 32
N_SUB = CHUNK // SUB

F32 = jnp.float32
BF16 = jnp.bfloat16


def _rms_norm_rows(x, w):
    ms = jnp.mean(x * x, axis=-1, keepdims=True)
    return x * lax.rsqrt(ms + EPS) * w


def _sigmoid(x):
    return 1.0 / (1.0 + jnp.exp(-x))


def _silu(x):
    return x * _sigmoid(x)


def _inproj_kernel(x_ref, nw_ref, w_ref, o_ref, h_scr):
    @pl.when(pl.program_id(1) == 0)
    def _():
        h_scr[...] = _rms_norm_rows(x_ref[...], nw_ref[...]).astype(BF16)

    o_ref[...] = jnp.dot(h_scr[...], w_ref[...],
                         preferred_element_type=F32).astype(o_ref.dtype)


def _inproj(x, norm_w, w_in_bf16):
    t = x.shape[0]
    n = w_in_bf16.shape[1]
    return pl.pallas_call(
        _inproj_kernel,
        out_shape=jax.ShapeDtypeStruct((t, n), BF16),
        grid=(t // TM_PROJ, n // D_MODEL),
        in_specs=[
            pl.BlockSpec((TM_PROJ, D_MODEL), lambda i, j: (i, 0)),
            pl.BlockSpec((1, D_MODEL), lambda i, j: (0, 0)),
            pl.BlockSpec((D_MODEL, D_MODEL), lambda i, j: (0, j)),
        ],
        out_specs=pl.BlockSpec((TM_PROJ, D_MODEL), lambda i, j: (i, j)),
        scratch_shapes=[pltpu.VMEM((TM_PROJ, D_MODEL), BF16)],
        compiler_params=pltpu.CompilerParams(
            dimension_semantics=("arbitrary", "arbitrary"),
            vmem_limit_bytes=VMEM_LIMIT),
        name="inproj",
    )(x, norm_w, w_in_bf16)


def _layer_lower_bound(lb_raw, layer):
    m = jnp.max(lb_raw, axis=0, keepdims=True)
    e = jnp.exp(lb_raw - m)
    soft = e / jnp.sum(e, axis=0, keepdims=True)
    acc = soft[0:1, :]
    for j in range(1, layer + 1):
        acc = acc + soft[j:j + 1, :]
    return acc - soft[0:1, :]


def _rec_chunk_head(q_raw, z, v_bf, g_raw, lb, gw, st, tri_f32, tri_bf):
    qh = _silu(q_raw) * HGRN_SCALE
    e = jnp.exp(-jnp.abs(z))
    r = 1.0 / (1.0 + e)
    er = e * r
    pos = z >= 0.0
    sig_pos = jnp.where(pos, r, er)
    sig_neg = jnp.where(pos, er, r)
    one_m_lb = 1.0 - lb
    f = lb + one_m_lb * sig_pos
    logf = jnp.log(jnp.maximum(f, F_MIN))
    kk = one_m_lb * sig_neg

    hi = logf.astype(BF16)
    lo = (logf - hi.astype(F32)).astype(BF16)
    cum2 = jnp.dot(tri_bf, jnp.concatenate([hi, lo], axis=1),
                   preferred_element_type=F32)
    cum = cum2[:, :HEAD_DIM] + cum2[:, HEAD_DIM:]

    s_ref = [jnp.zeros((1, HEAD_DIM), F32)]
    for j in range(1, N_SUB):
        s_ref.append(cum[j * SUB - 1:j * SUB, :])
    cum_end = cum[CHUNK - 1:CHUNK, :]

    q_loc, k_loc = [], []
    for j in range(N_SUB):
        rows = slice(j * SUB, (j + 1) * SUB)
        q_loc.append(qh[rows] * jnp.exp(cum[rows] - s_ref[j]))
        k_loc.append(kk[rows] * jnp.exp(s_ref[j] - cum[rows]))

    zeros_blk = jnp.zeros((SUB, HEAD_DIM), BF16)
    lhs_segs, rhs_segs = [], []
    q0_rows = []
    for j in range(N_SUB):
        lhs_rows, rhs_rows = [], []
        for i in range(N_SUB):
            if i < j:
                lhs_rows.append(zeros_blk)
            elif i == j:
                lhs_rows.append(q_loc[i].astype(BF16))
            else:
                lhs_rows.append((q_loc[i] * jnp.exp(s_ref[i] - s_ref[j])).astype(BF16))
            rhs_rows.append(k_loc[i].astype(BF16) if i == j else zeros_blk)
        if j == 0:
            q0_rows = lhs_rows
        lhs_segs.append(jnp.concatenate(lhs_rows, axis=0))
        rhs_segs.append(jnp.concatenate(rhs_rows, axis=0))
    lhs = jnp.concatenate(lhs_segs, axis=1)
    rhs = jnp.concatenate(rhs_segs, axis=1)
    scores = lax.dot_general(lhs, rhs, (((1,), (1,)), ((), ())),
                             preferred_element_type=F32) * tri_f32
    q_inter = lhs_segs[0]
    o = (jnp.dot(scores.astype(BF16), v_bf, preferred_element_type=F32)
         + lax.dot_general(q_inter, st.astype(BF16), (((1,), (1,)), ((), ())),
                           preferred_element_type=F32))

    k_state = jnp.concatenate(
        [(k_loc[j] * jnp.exp(cum_end - s_ref[j])).astype(BF16) for j in range(N_SUB)],
        axis=0)
    st_new = st * jnp.exp(cum_end) + lax.dot_general(
        v_bf, k_state, (((0,), (0,)), ((), ())), preferred_element_type=F32)

    out = _rms_norm_rows(o, gw) * _silu(g_raw)
    return out, st_new


def _rec_kernel(layer, q_ref, f_ref, i_ref, g_ref, lbraw_ref, gw_ref, tri_ref,
                o_ref, st_ref):
    @pl.when(pl.program_id(1) == 0)
    def _():
        st_ref[...] = jnp.zeros_like(st_ref)

    lb_all = _layer_lower_bound(lbraw_ref[...], layer)
    gw = gw_ref[...]
    tri_f32 = tri_ref[...]
    tri_bf = tri_f32.astype(BF16)

    def chunk_body(c, carry):
        r0 = pl.multiple_of(c * CHUNK, CHUNK)
        rows = pl.ds(r0, CHUNK)
        for h in range(HEADS):
            cols = slice(h * HEAD_DIM, (h + 1) * HEAD_DIM)
            out, st_new = _rec_chunk_head(
                q_ref[rows, cols].astype(F32), f_ref[rows, cols].astype(F32),
                i_ref[rows, cols], g_ref[rows, cols].astype(F32),
                lb_all[:, cols], gw, st_ref[h], tri_f32, tri_bf)
            st_ref[h] = st_new
            o_ref[rows, cols] = out.astype(o_ref.dtype)
        return carry

    lax.fori_loop(0, CT_REC // CHUNK, chunk_body, 0)


def _recurrence(proj, lower_bounds, gnorm_w, tri, layer, batch, seq):
    t = proj.shape[0]
    steps = seq // CT_REC

    def group_spec(g):
        return pl.BlockSpec((CT_REC, D_MODEL), lambda b, s: (b * steps + s, g))

    return pl.pallas_call(
        functools.partial(_rec_kernel, layer),
        out_shape=jax.ShapeDtypeStruct((t, D_MODEL), BF16),
        grid=(batch, steps),
        in_specs=[
            group_spec(G_Q), group_spec(G_F), group_spec(G_I), group_spec(G_G),
            pl.BlockSpec(lower_bounds.shape, lambda b, s: (0, 0)),
            pl.BlockSpec((1, HEAD_DIM), lambda b, s: (0, 0)),
            pl.BlockSpec((CHUNK, CHUNK), lambda b, s: (0, 0)),
        ],
        out_specs=pl.BlockSpec((CT_REC, D_MODEL), lambda b, s: (b * steps + s, 0)),
        scratch_shapes=[pltpu.VMEM((HEADS, HEAD_DIM, HEAD_DIM), F32)],
        compiler_params=pltpu.CompilerParams(
            dimension_semantics=("arbitrary", "arbitrary"),
            vmem_limit_bytes=VMEM_LIMIT),
        name="hgrn2_recurrence",
    )(proj, proj, proj, proj, lower_bounds, gnorm_w, tri)


def _mixout_kernel(seq, x_ref, oa_ref, b_ref, c_ref, u_ref, ga_ref, gb_ref,
                   bh_ref, uh_ref, cw_ref, wph_ref, wpc_ref, wo_ref, o_ref):
    i = pl.program_id(0)
    v = b_ref[...].astype(F32) * u_ref[...].astype(F32)
    at_seq_start = (i * TM_MIX) % seq == 0
    halo = bh_ref[...].astype(F32) * uh_ref[...].astype(F32)
    halo = jnp.where(at_seq_start, 0.0, halo)
    prev1 = halo[BF16_SUBLANES - 1:BF16_SUBLANES, :]
    prev2 = halo[BF16_SUBLANES - 2:BF16_SUBLANES - 1, :]
    row = lax.broadcasted_iota(jnp.int32, v.shape, 0)
    v1 = jnp.where(row == 0, prev1, pltpu.roll(v, 1, 0))
    v2 = jnp.where(row == 0, prev2,
                   jnp.where(row == 1, prev1, pltpu.roll(v, 2, 0)))
    cw = cw_ref[...]
    conv = cw[0:1, :] * v2 + cw[1:2, :] * v1 + cw[2:3, :] * v
    yb_in = (c_ref[...].astype(F32) * conv).astype(BF16)
    y_b = jnp.dot(yb_in, wpc_ref[...], preferred_element_type=F32)
    y_a = jnp.dot(oa_ref[...], wph_ref[...], preferred_element_type=F32)
    merged = (_sigmoid(ga_ref[...].astype(F32)) * y_a
              + _sigmoid(gb_ref[...].astype(F32)) * y_b).astype(BF16)
    o_ref[...] = x_ref[...] + jnp.dot(merged, wo_ref[...], preferred_element_type=F32)


def _mixout(x, o_gated, proj, conv_w, wph, wpc, wo, seq):
    t = x.shape[0]
    halo_per_tile = TM_MIX // BF16_SUBLANES

    def group_spec(g):
        return pl.BlockSpec((TM_MIX, D_MODEL), lambda i: (i, g))

    def halo_spec(g):
        return pl.BlockSpec(
            (BF16_SUBLANES, D_MODEL),
            lambda i: (jnp.maximum(i * halo_per_tile - 1, 0), g))

    def weight_spec():
        return pl.BlockSpec((D_MODEL, D_MODEL), lambda i: (0, 0))

    return pl.pallas_call(
        functools.partial(_mixout_kernel, seq),
        out_shape=jax.ShapeDtypeStruct((t, D_MODEL), F32),
        grid=(t // TM_MIX,),
        in_specs=[
            pl.BlockSpec((TM_MIX, D_MODEL), lambda i: (i, 0)),
            pl.BlockSpec((TM_MIX, D_MODEL), lambda i: (i, 0)),
            group_spec(G_B), group_spec(G_C), group_spec(G_U),
            group_spec(G_GA), group_spec(G_GB),
            halo_spec(G_B), halo_spec(G_U),
            pl.BlockSpec((CONV_K, D_MODEL), lambda i: (0, 0)),
            weight_spec(), weight_spec(), weight_spec(),
        ],
        out_specs=pl.BlockSpec((TM_MIX, D_MODEL), lambda i: (i, 0)),
        compiler_params=pltpu.CompilerParams(
            dimension_semantics=("arbitrary",),
            vmem_limit_bytes=VMEM_LIMIT),
        name="mixer_out",
    )(x, o_gated, proj, proj, proj, proj, proj, proj, proj, conv_w, wph, wpc, wo)


def _dense_ffn_kernel(x_ref, nw_ref, w1_ref, w3_ref, w2_ref, o_ref):
    x = x_ref[...]
    h = _rms_norm_rows(x, nw_ref[...]).astype(BF16)
    a = jnp.dot(h, w1_ref[...], preferred_element_type=F32)
    b = jnp.dot(h, w3_ref[...], preferred_element_type=F32)
    act = (_silu(a) * b).astype(BF16)
    o_ref[...] = x + jnp.dot(act, w2_ref[...], preferred_element_type=F32)


def _dense_ffn(x, norm_w, w1, w3, w2):
    t = x.shape[0]
    return pl.pallas_call(
        _dense_ffn_kernel,
        out_shape=jax.ShapeDtypeStruct((t, D_MODEL), F32),
        grid=(t // TM_FFN,),
        in_specs=[
            pl.BlockSpec((TM_FFN, D_MODEL), lambda i: (i, 0)),
            pl.BlockSpec((1, D_MODEL), lambda i: (0, 0)),
            pl.BlockSpec((D_MODEL, D_FF), lambda i: (0, 0)),
            pl.BlockSpec((D_MODEL, D_FF), lambda i: (0, 0)),
            pl.BlockSpec((D_FF, D_MODEL), lambda i: (0, 0)),
        ],
        out_specs=pl.BlockSpec((TM_FFN, D_MODEL), lambda i: (i, 0)),
        compiler_params=pltpu.CompilerParams(
            dimension_semantics=("arbitrary",),
            vmem_limit_bytes=VMEM_LIMIT),
        name="dense_ffn",
    )(x, norm_w, w1, w3, w2)


def _top2_gates(logits):
    lane = lax.broadcasted_iota(jnp.int32, logits.shape, 1)
    m1 = jnp.max(logits, axis=-1, keepdims=True)
    i1 = jnp.min(jnp.where(logits == m1, lane, LANES), axis=-1, keepdims=True)
    rest = jnp.where(lane == i1, -jnp.inf, logits)
    m2 = jnp.max(rest, axis=-1, keepdims=True)
    i2 = jnp.min(jnp.where(rest == m2, lane, LANES), axis=-1, keepdims=True)
    e2 = jnp.exp(m2 - m1)
    w1 = 1.0 / (1.0 + e2)
    w2 = e2 * w1
    return jnp.where(lane == i1, w1, 0.0) + jnp.where(lane == i2, w2, 0.0)


def _moe_ffn_kernel(x_ref, nw_ref, rw_ref, w1_ref, w3_ref, w2_ref, o_ref,
                    h_scr, gate_scr, acc_scr):
    e = pl.program_id(1)
    fc = pl.program_id(2)

    @pl.when(jnp.logical_and(e == 0, fc == 0))
    def _():
        h = _rms_norm_rows(x_ref[...], nw_ref[...])
        h_hi = h.astype(BF16)
        h_lo = (h - h_hi.astype(F32)).astype(BF16)
        rw = rw_ref[...]
        rw_hi = rw.astype(BF16)
        rw_lo = (rw - rw_hi.astype(F32)).astype(BF16)
        logits = (jnp.dot(h_hi, rw_hi, preferred_element_type=F32)
                  + jnp.dot(h_hi, rw_lo, preferred_element_type=F32)
                  + jnp.dot(h_lo, rw_hi, preferred_element_type=F32))
        lane = lax.broadcasted_iota(jnp.int32, logits.shape, 1)
        logits = jnp.where(lane < N_EXPERTS, logits, -jnp.inf)
        gate_scr[...] = _top2_gates(logits)
        h_scr[...] = h_hi
        acc_scr[...] = jnp.zeros_like(acc_scr)

    h = h_scr[...]
    a = jnp.dot(h, w1_ref[0], preferred_element_type=F32)
    b = jnp.dot(h, w3_ref[0], preferred_element_type=F32)
    act = (_silu(a) * b).astype(BF16)
    y = jnp.dot(act, w2_ref[0], preferred_element_type=F32)
    gates = gate_scr[...]
    lane = lax.broadcasted_iota(jnp.int32, gates.shape, 1)
    g_col = jnp.sum(jnp.where(lane == e, gates, 0.0), axis=-1, keepdims=True)
    acc_scr[...] += g_col * y

    @pl.when(jnp.logical_and(e == pl.num_programs(1) - 1,
                             fc == pl.num_programs(2) - 1))
    def _():
        o_ref[...] = x_ref[...] + acc_scr[...]


def _moe_ffn(x, norm_w, router_w_padded, w1, w3, w2):
    t = x.shape[0]
    n_fc = D_FF // FF_CHUNK
    return pl.pallas_call(
        _moe_ffn_kernel,
        out_shape=jax.ShapeDtypeStruct((t, D_MODEL), F32),
        grid=(t // TM_FFN, N_EXPERTS, n_fc),
        in_specs=[
            pl.BlockSpec((TM_FFN, D_MODEL), lambda i, e, f: (i, 0)),
            pl.BlockSpec((1, D_MODEL), lambda i, e, f: (0, 0)),
            pl.BlockSpec((D_MODEL, LANES), lambda i, e, f: (0, 0)),
            pl.BlockSpec((1, D_MODEL, FF_CHUNK), lambda i, e, f: (e, 0, f)),
            pl.BlockSpec((1, D_MODEL, FF_CHUNK), lambda i, e, f: (e, 0, f)),
            pl.BlockSpec((1, FF_CHUNK, D_MODEL), lambda i, e, f: (e, f, 0)),
        ],
        out_specs=pl.BlockSpec((TM_FFN, D_MODEL), lambda i, e, f: (i, 0)),
        scratch_shapes=[
            pltpu.VMEM((TM_FFN, D_MODEL), BF16),
            pltpu.VMEM((TM_FFN, LANES), F32),
            pltpu.VMEM((TM_FFN, D_MODEL), F32),
        ],
        compiler_params=pltpu.CompilerParams(
            dimension_semantics=("arbitrary", "arbitrary", "arbitrary"),
            vmem_limit_bytes=VMEM_LIMIT),
        name="moe_ffn",
    )(x, norm_w, router_w_padded, w1, w3, w2)


def _final_norm_kernel(x_ref, nw_ref, o_ref):
    o_ref[...] = _rms_norm_rows(x_ref[...], nw_ref[...])


def _final_norm(x, norm_w):
    t = x.shape[0]
    return pl.pallas_call(
        _final_norm_kernel,
        out_shape=jax.ShapeDtypeStruct((t, D_MODEL), F32),
        grid=(t // TM_PROJ,),
        in_specs=[
            pl.BlockSpec((TM_PROJ, D_MODEL), lambda i: (i, 0)),
            pl.BlockSpec((1, D_MODEL), lambda i: (0, 0)),
        ],
        out_specs=pl.BlockSpec((TM_PROJ, D_MODEL), lambda i: (i, 0)),
        compiler_params=pltpu.CompilerParams(dimension_semantics=("arbitrary",)),
        name="final_norm",
    )(x, norm_w)


def kernel(x, w_in, lower_bounds, hgrn_norm_w, conv_w, w_proj_hgrn, w_proj_conv, w_out,
           norm_mix, norm_ffn, dense_w1, dense_w3, dense_w2, router_w,
           expert_w1, expert_w3, expert_w2, final_norm):
    batch, seq, d = x.shape
    depth = w_in.shape[0]
    assert d == D_MODEL and seq % CT_REC == 0 and (batch * seq) % TM_PROJ == 0
    assert w_in.shape[2] == N_GROUPS * D_MODEL

    xt = x.reshape(batch * seq, d)
    tri = jnp.tril(jnp.ones((CHUNK, CHUNK), F32))
    lower_bounds = lower_bounds.astype(F32)

    for layer in range(depth):
        proj = _inproj(xt, norm_mix[layer].reshape(1, d), w_in[layer].astype(BF16))
        o_gated = _recurrence(proj, lower_bounds, hgrn_norm_w[layer].reshape(1, HEAD_DIM),
                              tri, layer, batch, seq)
        xt = _mixout(xt, o_gated, proj, conv_w[layer],
                     w_proj_hgrn[layer].astype(BF16), w_proj_conv[layer].astype(BF16),
                     w_out[layer].astype(BF16), seq)
        j = layer // 2
        nw = norm_ffn[layer].reshape(1, d)
        if layer % 2 == 0:
            xt = _dense_ffn(xt, nw, dense_w1[j].astype(BF16), dense_w3[j].astype(BF16),
                            dense_w2[j].astype(BF16))
        else:
            rw = jnp.pad(router_w[j].astype(F32), ((0, 0), (0, LANES - N_EXPERTS)))
            xt = _moe_ffn(xt, nw, rw, expert_w1[j].astype(BF16), expert_w3[j].astype(BF16),
                          expert_w2[j].astype(BF16))
    out = _final_norm(xt, final_norm.reshape(1, d))
    return out.reshape(batch, seq, d)
```

```python
import functools

import jax
import jax.numpy as jnp
from jax import lax
from jax.experimental import pallas as pl
from jax.experimental.pallas import tpu as pltpu

D_MODEL = 1024
HEADS = 8
HEAD_DIM = 128
HGRN_SCALE = HEAD_DIM ** -0.5
F_MIN = 1e-6
CONV_K = 3
D_FF = 2816
N_EXPERTS = 8
TOP_K = 2
EPS = 1e-6
N_GROUPS = 9
G_Q, G_F, G_I, G_G, G_B, G_C, G_U, G_GA, G_GB = range(N_GROUPS)

LANES = 128
SUBLANES = 8
BF16_SUBLANES = 16
VMEM_LIMIT = 56 * 1024 * 1024

TM_PROJ = 1024
TM_MIX = 512
TM_FFN = 512
FF_CHUNK = D_FF // 2
TM_ROUTE = 512
TM_DISP = 512
TM_GRP = 512
TM_COMB = 256
DMA_ISSUE_UNROLL = 8
CT_REC = 1024
CHUNK = 128
SUB = 32
N_SUB = CHUNK // SUB

F32 = jnp.float32
BF16 = jnp.bfloat16
I32 = jnp.int32


def _rms_norm_rows(x, w):
    ms = jnp.mean(x * x, axis=-1, keepdims=True)
    return x * lax.rsqrt(ms + EPS) * w


def _sigmoid(x):
    return 1.0 / (1.0 + jnp.exp(-x))


def _silu(x):
    return x * _sigmoid(x)


def _split_bf16(v):
    hi = v.astype(BF16)
    lo = (v - hi.astype(F32)).astype(BF16)
    return hi, lo


def _dot3(a_hi, a_lo, b_hi, b_lo, dims):
    dg = functools.partial(lax.dot_general, dimension_numbers=dims,
                           preferred_element_type=F32)
    return dg(a_hi, b_hi) + dg(a_hi, b_lo) + dg(a_lo, b_hi)


def _inproj_kernel(x_ref, nw_ref, w_ref, o_ref, h_scr):
    @pl.when(pl.program_id(1) == 0)
    def _():
        h_scr[...] = _rms_norm_rows(x_ref[...], nw_ref[...]).astype(BF16)

    o_ref[...] = jnp.dot(h_scr[...], w_ref[...],
                         preferred_element_type=F32).astype(o_ref.dtype)


def _inproj(x, norm_w, w_in_bf16):
    t = x.shape[0]
    n = w_in_bf16.shape[1]
    return pl.pallas_call(
        _inproj_kernel,
        out_shape=jax.ShapeDtypeStruct((t, n), BF16),
        grid=(t // TM_PROJ, n // D_MODEL),
        in_specs=[
            pl.BlockSpec((TM_PROJ, D_MODEL), lambda i, j: (i, 0)),
            pl.BlockSpec((1, D_MODEL), lambda i, j: (0, 0)),
            pl.BlockSpec((D_MODEL, D_MODEL), lambda i, j: (0, j)),
        ],
        out_specs=pl.BlockSpec((TM_PROJ, D_MODEL), lambda i, j: (i, j)),
        scratch_shapes=[pltpu.VMEM((TM_PROJ, D_MODEL), BF16)],
        compiler_params=pltpu.CompilerParams(
            dimension_semantics=("arbitrary", "arbitrary"),
            vmem_limit_bytes=VMEM_LIMIT),
        name="inproj",
    )(x, norm_w, w_in_bf16)


def _layer_lower_bound(lb_raw, layer):
    m = jnp.max(lb_raw, axis=0, keepdims=True)
    e = jnp.exp(lb_raw - m)
    soft = e / jnp.sum(e, axis=0, keepdims=True)
    acc = soft[0:1, :]
    for j in range(1, layer + 1):
        acc = acc + soft[j:j + 1, :]
    return acc - soft[0:1, :]


def _rec_chunk_head(q_raw, z, v_bf, g_raw, lb, gw, st, tri_f32, tri_bf):
    qh = _silu(q_raw) * HGRN_SCALE
    e = jnp.exp(-jnp.abs(z))
    r = 1.0 / (1.0 + e)
    er = e * r
    pos = z >= 0.0
    sig_pos = jnp.where(pos, r, er)
    sig_neg = jnp.where(pos, er, r)
    one_m_lb = 1.0 - lb
    f = lb + one_m_lb * sig_pos
    logf = jnp.log(jnp.maximum(f, F_MIN))
    kk = one_m_lb * sig_neg

    hi, lo = _split_bf16(logf)
    cum2 = jnp.dot(tri_bf, jnp.concatenate([hi, lo], axis=1),
                   preferred_element_type=F32)
    cum = cum2[:, :HEAD_DIM] + cum2[:, HEAD_DIM:]

    s_ref = [jnp.zeros((1, HEAD_DIM), F32)]
    for j in range(1, N_SUB):
        s_ref.append(cum[j * SUB - 1:j * SUB, :])
    cum_end = cum[CHUNK - 1:CHUNK, :]

    q_loc, k_loc = [], []
    for j in range(N_SUB):
        rows = slice(j * SUB, (j + 1) * SUB)
        q_loc.append(qh[rows] * jnp.exp(cum[rows] - s_ref[j]))
        k_loc.append(kk[rows] * jnp.exp(s_ref[j] - cum[rows]))

    zeros_blk = jnp.zeros((SUB, HEAD_DIM), BF16)
    lhs_segs, rhs_segs = [], []
    for j in range(N_SUB):
        lhs_rows, rhs_rows = [], []
        for i in range(N_SUB):
            if i < j:
                lhs_rows.append(zeros_blk)
            elif i == j:
                lhs_rows.append(q_loc[i].astype(BF16))
            else:
                lhs_rows.append((q_loc[i] * jnp.exp(s_ref[i] - s_ref[j])).astype(BF16))
            rhs_rows.append(k_loc[i].astype(BF16) if i == j else zeros_blk)
        lhs_segs.append(jnp.concatenate(lhs_rows, axis=0))
        rhs_segs.append(jnp.concatenate(rhs_rows, axis=0))
    lhs = jnp.concatenate(lhs_segs, axis=1)
    rhs = jnp.concatenate(rhs_segs, axis=1)
    scores = lax.dot_general(lhs, rhs, (((1,), (1,)), ((), ())),
                             preferred_element_type=F32) * tri_f32
    q_inter = lhs_segs[0]
    o = (jnp.dot(scores.astype(BF16), v_bf, preferred_element_type=F32)
         + lax.dot_general(q_inter, st.astype(BF16), (((1,), (1,)), ((), ())),
                           preferred_element_type=F32))

    k_state = jnp.concatenate(
        [(k_loc[j] * jnp.exp(cum_end - s_ref[j])).astype(BF16) for j in range(N_SUB)],
        axis=0)
    st_new = st * jnp.exp(cum_end) + lax.dot_general(
        v_bf, k_state, (((0,), (0,)), ((), ())), preferred_element_type=F32)

    out = _rms_norm_rows(o, gw) * _silu(g_raw)
    return out, st_new


def _rec_kernel(layer, q_ref, f_ref, i_ref, g_ref, lbraw_ref, gw_ref, tri_ref,
                o_ref, st_ref):
    @pl.when(pl.program_id(1) == 0)
    def _():
        st_ref[...] = jnp.zeros_like(st_ref)

    lb_all = _layer_lower_bound(lbraw_ref[...], layer)
    gw = gw_ref[...]
    tri_f32 = tri_ref[...]
    tri_bf = tri_f32.astype(BF16)

    def chunk_body(c, carry):
        r0 = pl.multiple_of(c * CHUNK, CHUNK)
        rows = pl.ds(r0, CHUNK)
        for h in range(HEADS):
            cols = slice(h * HEAD_DIM, (h + 1) * HEAD_DIM)
            out, st_new = _rec_chunk_head(
                q_ref[rows, cols].astype(F32), f_ref[rows, cols].astype(F32),
                i_ref[rows, cols], g_ref[rows, cols].astype(F32),
                lb_all[:, cols], gw, st_ref[h], tri_f32, tri_bf)
            st_ref[h] = st_new
            o_ref[rows, cols] = out.astype(o_ref.dtype)
        return carry

    lax.fori_loop(0, CT_REC // CHUNK, chunk_body, 0)


def _recurrence(proj, lower_bounds, gnorm_w, tri, layer, batch, seq):
    t = proj.shape[0]
    steps = seq // CT_REC

    def group_spec(g):
        return pl.BlockSpec((CT_REC, D_MODEL), lambda b, s: (b * steps + s, g))

    return pl.pallas_call(
        functools.partial(_rec_kernel, layer),
        out_shape=jax.ShapeDtypeStruct((t, D_MODEL), BF16),
        grid=(batch, steps),
        in_specs=[
            group_spec(G_Q), group_spec(G_F), group_spec(G_I), group_spec(G_G),
            pl.BlockSpec(lower_bounds.shape, lambda b, s: (0, 0)),
            pl.BlockSpec((1, HEAD_DIM), lambda b, s: (0, 0)),
            pl.BlockSpec((CHUNK, CHUNK), lambda b, s: (0, 0)),
        ],
        out_specs=pl.BlockSpec((CT_REC, D_MODEL), lambda b, s: (b * steps + s, 0)),
        scratch_shapes=[pltpu.VMEM((HEADS, HEAD_DIM, HEAD_DIM), F32)],
        compiler_params=pltpu.CompilerParams(
            dimension_semantics=("arbitrary", "arbitrary"),
            vmem_limit_bytes=VMEM_LIMIT),
        name="hgrn2_recurrence",
    )(proj, proj, proj, proj, lower_bounds, gnorm_w, tri)


def _mixout_kernel(seq, x_ref, oa_ref, b_ref, c_ref, u_ref, ga_ref, gb_ref,
                   bh_ref, uh_ref, cw_ref, wph_ref, wpc_ref, wo_ref, o_ref):
    i = pl.program_id(0)
    v = b_ref[...].astype(F32) * u_ref[...].astype(F32)
    at_seq_start = (i * TM_MIX) % seq == 0
    halo = bh_ref[...].astype(F32) * uh_ref[...].astype(F32)
    halo = jnp.where(at_seq_start, 0.0, halo)
    prev1 = halo[BF16_SUBLANES - 1:BF16_SUBLANES, :]
    prev2 = halo[BF16_SUBLANES - 2:BF16_SUBLANES - 1, :]
    row = lax.broadcasted_iota(I32, v.shape, 0)
    v1 = jnp.where(row == 0, prev1, pltpu.roll(v, 1, 0))
    v2 = jnp.where(row == 0, prev2,
                   jnp.where(row == 1, prev1, pltpu.roll(v, 2, 0)))
    cw = cw_ref[...]
    conv = cw[0:1, :] * v2 + cw[1:2, :] * v1 + cw[2:3, :] * v
    yb_in = (c_ref[...].astype(F32) * conv).astype(BF16)
    y_b = jnp.dot(yb_in, wpc_ref[...], preferred_element_type=F32)
    y_a = jnp.dot(oa_ref[...], wph_ref[...], preferred_element_type=F32)
    merged = (_sigmoid(ga_ref[...].astype(F32)) * y_a
              + _sigmoid(gb_ref[...].astype(F32)) * y_b).astype(BF16)
    o_ref[...] = x_ref[...] + jnp.dot(merged, wo_ref[...], preferred_element_type=F32)


def _mixout(x, o_gated, proj, conv_w, wph, wpc, wo, seq):
    t = x.shape[0]
    halo_per_tile = TM_MIX // BF16_SUBLANES

    def group_spec(g):
        return pl.BlockSpec((TM_MIX, D_MODEL), lambda i: (i, g))

    def halo_spec(g):
        return pl.BlockSpec(
            (BF16_SUBLANES, D_MODEL),
            lambda i: (jnp.maximum(i * halo_per_tile - 1, 0), g))

    def weight_spec():
        return pl.BlockSpec((D_MODEL, D_MODEL), lambda i: (0, 0))

    return pl.pallas_call(
        functools.partial(_mixout_kernel, seq),
        out_shape=jax.ShapeDtypeStruct((t, D_MODEL), F32),
        grid=(t // TM_MIX,),
        in_specs=[
            pl.BlockSpec((TM_MIX, D_MODEL), lambda i: (i, 0)),
            pl.BlockSpec((TM_MIX, D_MODEL), lambda i: (i, 0)),
            group_spec(G_B), group_spec(G_C), group_spec(G_U),
            group_spec(G_GA), group_spec(G_GB),
            halo_spec(G_B), halo_spec(G_U),
            pl.BlockSpec((CONV_K, D_MODEL), lambda i: (0, 0)),
            weight_spec(), weight_spec(), weight_spec(),
        ],
        out_specs=pl.BlockSpec((TM_MIX, D_MODEL), lambda i: (i, 0)),
        compiler_params=pltpu.CompilerParams(
            dimension_semantics=("arbitrary",),
            vmem_limit_bytes=VMEM_LIMIT),
        name="mixer_out",
    )(x, o_gated, proj, proj, proj, proj, proj, proj, proj, conv_w, wph, wpc, wo)


def _dense_ffn_kernel(x_ref, nw_ref, w1_ref, w3_ref, w2_ref, o_ref):
    x = x_ref[...]
    h = _rms_norm_rows(x, nw_ref[...]).astype(BF16)
    a = jnp.dot(h, w1_ref[...], preferred_element_type=F32)
    b = jnp.dot(h, w3_ref[...], preferred_element_type=F32)
    act = (_silu(a) * b).astype(BF16)
    o_ref[...] = x + jnp.dot(act, w2_ref[...], preferred_element_type=F32)


def _dense_ffn(x, norm_w, w1, w3, w2):
    t = x.shape[0]
    return pl.pallas_call(
        _dense_ffn_kernel,
        out_shape=jax.ShapeDtypeStruct((t, D_MODEL), F32),
        grid=(t // TM_FFN,),
        in_specs=[
            pl.BlockSpec((TM_FFN, D_MODEL), lambda i: (i, 0)),
            pl.BlockSpec((1, D_MODEL), lambda i: (0, 0)),
            pl.BlockSpec((D_MODEL, D_FF), lambda i: (0, 0)),
            pl.BlockSpec((D_MODEL, D_FF), lambda i: (0, 0)),
            pl.BlockSpec((D_FF, D_MODEL), lambda i: (0, 0)),
        ],
        out_specs=pl.BlockSpec((TM_FFN, D_MODEL), lambda i: (i, 0)),
        compiler_params=pltpu.CompilerParams(
            dimension_semantics=("arbitrary",),
            vmem_limit_bytes=VMEM_LIMIT),
        name="dense_ffn",
    )(x, norm_w, w1, w3, w2)


def _route_kernel(x_ref, nw_ref, rwt_ref, rwp_ref, triu_ref,
                  h_ref, meta_ref, wts_ref, cnt_ref, carry_scr):
    @pl.when(pl.program_id(0) == 0)
    def _():
        carry_scr[...] = jnp.zeros_like(carry_scr)

    h = _rms_norm_rows(x_ref[...], nw_ref[...])
    h_ref[...] = h
    h_hi, h_lo = _split_bf16(h)

    rwt_hi, rwt_lo = _split_bf16(rwt_ref[...])
    lt = _dot3(rwt_hi, rwt_lo, h_hi, h_lo, (((1,), (1,)), ((), ())))
    sub = lax.broadcasted_iota(I32, lt.shape, 0)
    m1 = jnp.max(lt, axis=0, keepdims=True)
    i1 = jnp.min(jnp.where(lt == m1, sub, N_EXPERTS), axis=0, keepdims=True)
    rest = jnp.where(sub == i1, -jnp.inf, lt)
    m2 = jnp.max(rest, axis=0, keepdims=True)
    i2 = jnp.min(jnp.where(rest == m2, sub, N_EXPERTS), axis=0, keepdims=True)
    oh1 = sub == i1
    oh2 = sub == i2
    member = jnp.where(jnp.logical_or(oh1, oh2), 1.0, 0.0)
    prefix = jnp.dot(member.astype(BF16), triu_ref[...], preferred_element_type=F32)
    carry = carry_scr[...]
    base = carry[:, 0:1] + prefix
    rank1 = jnp.sum(jnp.where(oh1, base, 0.0), axis=0, keepdims=True)
    rank2 = jnp.sum(jnp.where(oh2, base, 0.0), axis=0, keepdims=True)
    carry_new = carry + jnp.sum(member, axis=1, keepdims=True)
    carry_scr[...] = carry_new
    cnt_ref[...] = carry_new.astype(I32)
    meta_ref[...] = jnp.where(
        sub == 0, i1, jnp.where(
            sub == 1, i2, jnp.where(
                sub == 2, rank1.astype(I32), jnp.where(
                    sub == 3, rank2.astype(I32), 0))))

    rwp_hi, rwp_lo = _split_bf16(rwp_ref[...])
    lr = _dot3(h_hi, h_lo, rwp_hi, rwp_lo, (((1,), (0,)), ((), ())))
    lane = lax.broadcasted_iota(I32, lr.shape, 1)
    lr = jnp.where(lane < N_EXPERTS, lr, -jnp.inf)
    m1r = jnp.max(lr, axis=-1, keepdims=True)
    i1r = jnp.min(jnp.where(lr == m1r, lane, LANES), axis=-1, keepdims=True)
    m2r = jnp.max(jnp.where(lane == i1r, -jnp.inf, lr), axis=-1, keepdims=True)
    e2 = jnp.exp(m2r - m1r)
    w1 = 1.0 / (1.0 + e2)
    w2 = e2 * w1
    wts_ref[...] = jnp.where(lane == 0, w1, jnp.where(lane == 1, w2, 0.0))


def _route(x, norm_w, router_w):
    t = x.shape[0]
    rwt = router_w.T
    rwp = jnp.pad(router_w, ((0, 0), (0, LANES - N_EXPERTS)))
    triu = jnp.triu(jnp.ones((TM_ROUTE, TM_ROUTE), BF16), k=1)
    return pl.pallas_call(
        _route_kernel,
        out_shape=(
            jax.ShapeDtypeStruct((t, D_MODEL), F32),
            jax.ShapeDtypeStruct((SUBLANES, t), I32),
            jax.ShapeDtypeStruct((t, LANES), F32),
            jax.ShapeDtypeStruct((N_EXPERTS, LANES), I32),
        ),
        grid=(t // TM_ROUTE,),
        in_specs=[
            pl.BlockSpec((TM_ROUTE, D_MODEL), lambda i: (i, 0)),
            pl.BlockSpec((1, D_MODEL), lambda i: (0, 0)),
            pl.BlockSpec((N_EXPERTS, D_MODEL), lambda i: (0, 0)),
            pl.BlockSpec((D_MODEL, LANES), lambda i: (0, 0)),
            pl.BlockSpec((TM_ROUTE, TM_ROUTE), lambda i: (0, 0)),
        ],
        out_specs=(
            pl.BlockSpec((TM_ROUTE, D_MODEL), lambda i: (i, 0)),
            pl.BlockSpec((SUBLANES, TM_ROUTE), lambda i: (0, i)),
            pl.BlockSpec((TM_ROUTE, LANES), lambda i: (i, 0)),
            pl.BlockSpec((N_EXPERTS, LANES), lambda i: (0, 0)),
        ),
        scratch_shapes=[pltpu.VMEM((N_EXPERTS, LANES), F32)],
        compiler_params=pltpu.CompilerParams(
            dimension_semantics=("arbitrary",),
            vmem_limit_bytes=VMEM_LIMIT),
        name="moe_route",
    )(x, norm_w, rwt, rwp, triu)


def _pos_copy(pos_hbm, pos_smem, sem, step, slot):
    return pltpu.make_async_copy(pos_hbm.at[step], pos_smem.at[slot], sem.at[slot])


def _dispatch_kernel(fill_ref, pos_hbm, h_hbm, xs_hbm, pos_smem, zbuf, psem, dsem):
    i = pl.program_id(0)
    n = pl.num_programs(0)
    slot = i % 2

    @pl.when(i == 0)
    def _():
        _pos_copy(pos_hbm, pos_smem, psem, 0, 0).start()

    _pos_copy(pos_hbm, pos_smem, psem, i, slot).wait()

    @pl.when(i + 1 < n)
    def _():
        _pos_copy(pos_hbm, pos_smem, psem, i + 1, 1 - slot).start()

    base = i * TM_DISP

    def issue(t, carry):
        for j in range(TOP_K):
            p = pos_smem[slot, j * TM_DISP + t]
            pltpu.make_async_copy(h_hbm.at[pl.ds(base + t, 1)],
                                  xs_hbm.at[pl.ds(p, 1)], dsem).start()
        return carry

    lax.fori_loop(0, TM_DISP, issue, 0, unroll=DMA_ISSUE_UNROLL)

    def wait_one_step_of_rows():
        for _ in range(TOP_K):
            pltpu.make_async_copy(h_hbm.at[pl.ds(0, TM_DISP)],
                                  xs_hbm.at[pl.ds(0, TM_DISP)], dsem).wait()

    @pl.when(i > 0)
    def _():
        wait_one_step_of_rows()

    @pl.when(i == n - 1)
    def _():
        wait_one_step_of_rows()
        zbuf[...] = jnp.zeros_like(zbuf)

        def zero_row(r, carry):
            cp = pltpu.make_async_copy(zbuf.at[pl.ds(0, 1)], xs_hbm.at[pl.ds(r, 1)], dsem)
            cp.start()
            cp.wait()
            return carry

        def zero_tile(k, carry):
            r0 = pl.multiple_of(k * TM_GRP, TM_GRP)
            cp = pltpu.make_async_copy(zbuf, xs_hbm.at[pl.ds(r0, TM_GRP)], dsem)
            cp.start()
            cp.wait()
            return carry

        for e in range(N_EXPERTS):
            lax.fori_loop(fill_ref[e], fill_ref[N_EXPERTS + e], zero_row, 0)
        lax.fori_loop(fill_ref[2 * N_EXPERTS], xs_hbm.shape[0] // TM_GRP, zero_tile, 0)


def _dispatch(h, pos_tiles, fill_bounds, n_sorted_rows):
    t = h.shape[0]
    return pl.pallas_call(
        _dispatch_kernel,
        out_shape=jax.ShapeDtypeStruct((n_sorted_rows, D_MODEL), F32),
        grid_spec=pltpu.PrefetchScalarGridSpec(
            num_scalar_prefetch=1,
            grid=(t // TM_DISP,),
            in_specs=[pl.BlockSpec(memory_space=pl.ANY),
                      pl.BlockSpec(memory_space=pl.ANY)],
            out_specs=pl.BlockSpec(memory_space=pl.ANY),
            scratch_shapes=[
                pltpu.SMEM((2, TOP_K * TM_DISP), I32),
                pltpu.VMEM((TM_GRP, D_MODEL), F32),
                pltpu.SemaphoreType.DMA((2,)),
                pltpu.SemaphoreType.DMA(()),
            ]),
        compiler_params=pltpu.CompilerParams(dimension_semantics=("arbitrary",)),
        name="moe_dispatch",
    )(fill_bounds, pos_tiles, h)


def _experts_kernel(te_ref, nv_ref, xs_ref, w1_ref, w3_ref, w2_ref, y_ref, xb_scr):
    k = pl.program_id(0)
    fc = pl.program_id(1)

    @pl.when(k < nv_ref[0])
    def _():
        @pl.when(fc == 0)
        def _():
            xb_scr[...] = xs_ref[...].astype(BF16)

        xb = xb_scr[...]
        a = jnp.dot(xb, w1_ref[0], preferred_element_type=F32)
        b = jnp.dot(xb, w3_ref[0], preferred_element_type=F32)
        act = (_silu(a) * b).astype(BF16)
        y = jnp.dot(act, w2_ref[0], preferred_element_type=F32)

        @pl.when(fc == 0)
        def _():
            y_ref[...] = y

        @pl.when(fc > 0)
        def _():
            y_ref[...] += y

    @pl.when(jnp.logical_and(k >= nv_ref[0], fc == 0))
    def _():
        y_ref[...] = jnp.zeros_like(y_ref)


def _experts(xs, tile_expert, n_valid, w1, w3, w2):
    n_rows = xs.shape[0]
    n_fc = D_FF // FF_CHUNK

    def row_map(k, f, te, nv):
        return (jnp.minimum(k, nv[0] - 1), 0)

    def out_map(k, f, te, nv):
        return (k, 0)

    def fc_of(k, f, nv):
        return jnp.where(k < nv[0], f, n_fc - 1)

    def w13_map(k, f, te, nv):
        return (te[jnp.minimum(k, nv[0] - 1)], 0, fc_of(k, f, nv))

    def w2_map(k, f, te, nv):
        return (te[jnp.minimum(k, nv[0] - 1)], fc_of(k, f, nv), 0)

    return pl.pallas_call(
        _experts_kernel,
        out_shape=jax.ShapeDtypeStruct((n_rows, D_MODEL), F32),
        grid_spec=pltpu.PrefetchScalarGridSpec(
            num_scalar_prefetch=2,
            grid=(n_rows // TM_GRP, n_fc),
            in_specs=[
                pl.BlockSpec((TM_GRP, D_MODEL), row_map),
                pl.BlockSpec((1, D_MODEL, FF_CHUNK), w13_map),
                pl.BlockSpec((1, D_MODEL, FF_CHUNK), w13_map),
                pl.BlockSpec((1, FF_CHUNK, D_MODEL), w2_map),
            ],
            out_specs=pl.BlockSpec((TM_GRP, D_MODEL), out_map),
            scratch_shapes=[pltpu.VMEM((TM_GRP, D_MODEL), BF16)]),
        compiler_params=pltpu.CompilerParams(
            dimension_semantics=("arbitrary", "arbitrary"),
            vmem_limit_bytes=VMEM_LIMIT),
        name="moe_experts",
    )(tile_expert, n_valid, xs, w1, w3, w2)


def _combine_kernel(fuse_norm, pos_hbm, y_hbm, x_ref, wts_ref, fnw_ref, o_ref,
                    pos_smem, ybuf, psem, gsem):
    i = pl.program_id(0)
    n = pl.num_programs(0)
    slot = i % 2

    def issue_gathers(s):
        def body(t, carry):
            for j in range(TOP_K):
                p = pos_smem[s, j * TM_COMB + t]
                pltpu.make_async_copy(y_hbm.at[pl.ds(p, 1)],
                                      ybuf.at[s, j, pl.ds(t, 1)], gsem.at[s]).start()
            return carry

        lax.fori_loop(0, TM_COMB, body, 0, unroll=DMA_ISSUE_UNROLL)

    @pl.when(i == 0)
    def _():
        first = _pos_copy(pos_hbm, pos_smem, psem, 0, 0)
        first.start()
        first.wait()
        issue_gathers(0)

        @pl.when(n > 1)
        def _():
            _pos_copy(pos_hbm, pos_smem, psem, 1, 1).start()

    @pl.when(i + 1 < n)
    def _():
        _pos_copy(pos_hbm, pos_smem, psem, i + 1, 1 - slot).wait()
        issue_gathers(1 - slot)

        @pl.when(i + 2 < n)
        def _():
            _pos_copy(pos_hbm, pos_smem, psem, i + 2, slot).start()

    for j in range(TOP_K):
        pltpu.make_async_copy(y_hbm.at[pl.ds(0, TM_COMB)], ybuf.at[slot, j],
                              gsem.at[slot]).wait()

    w = wts_ref[...]
    out = x_ref[...] + w[:, 0:1] * ybuf[slot, 0] + w[:, 1:2] * ybuf[slot, 1]
    if fuse_norm:
        out = _rms_norm_rows(out, fnw_ref[...])
    o_ref[...] = out


def _combine(x, y, pos_tiles, wts, final_norm_w, fuse_norm):
    t = x.shape[0]
    return pl.pallas_call(
        functools.partial(_combine_kernel, fuse_norm),
        out_shape=jax.ShapeDtypeStruct((t, D_MODEL), F32),
        grid=(t // TM_COMB,),
        in_specs=[
            pl.BlockSpec(memory_space=pl.ANY),
            pl.BlockSpec(memory_space=pl.ANY),
            pl.BlockSpec((TM_COMB, D_MODEL), lambda i: (i, 0)),
            pl.BlockSpec((TM_COMB, LANES), lambda i: (i, 0)),
            pl.BlockSpec((1, D_MODEL), lambda i: (0, 0)),
        ],
        out_specs=pl.BlockSpec((TM_COMB, D_MODEL), lambda i: (i, 0)),
        scratch_shapes=[
            pltpu.SMEM((2, TOP_K * TM_COMB), I32),
            pltpu.VMEM((2, TOP_K, TM_COMB, D_MODEL), F32),
            pltpu.SemaphoreType.DMA((2,)),
            pltpu.SemaphoreType.DMA((2,)),
        ],
        compiler_params=pltpu.CompilerParams(dimension_semantics=("arbitrary",)),
        name="moe_combine",
    )(pos_tiles, y, x, wts, final_norm_w)


def _tile_major(pos, tile):
    t = pos.shape[1]
    return pos.reshape(TOP_K, t // tile, tile).transpose(1, 0, 2).reshape(t // tile, TOP_K * tile)


def _moe_ffn(x, norm_w, router_w, w1, w3, w2, final_norm_w, fuse_norm):
    t = x.shape[0]
    h, meta, wts, cnt = _route(x, norm_w, router_w)

    counts = cnt[:, 0]
    padded = ((counts + TM_GRP - 1) // TM_GRP) * TM_GRP
    ends = jnp.cumsum(padded)
    offs = ends - padded
    n_sorted_rows = TOP_K * t + N_EXPERTS * TM_GRP
    tile_start = jnp.arange(n_sorted_rows // TM_GRP, dtype=I32) * TM_GRP
    tile_expert = jnp.minimum(
        jnp.sum((tile_start[:, None] >= ends[None, :]).astype(I32), axis=1), N_EXPERTS - 1)
    n_valid = (ends[-1] // TM_GRP).astype(I32).reshape(1)
    pos = jnp.stack([offs[meta[0]] + meta[2], offs[meta[1]] + meta[3]]).astype(I32)
    fill_bounds = jnp.concatenate([offs + counts, ends, n_valid]).astype(I32)

    xs = _dispatch(h, _tile_major(pos, TM_DISP), fill_bounds, n_sorted_rows)
    y = _experts(xs, tile_expert.astype(I32), n_valid, w1, w3, w2)
    return _combine(x, y, _tile_major(pos, TM_COMB), wts, final_norm_w, fuse_norm)


def _final_norm_kernel(x_ref, nw_ref, o_ref):
    o_ref[...] = _rms_norm_rows(x_ref[...], nw_ref[...])


def _final_norm(x, norm_w):
    t = x.shape[0]
    return pl.pallas_call(
        _final_norm_kernel,
        out_shape=jax.ShapeDtypeStruct((t, D_MODEL), F32),
        grid=(t // TM_PROJ,),
        in_specs=[
            pl.BlockSpec((TM_PROJ, D_MODEL), lambda i: (i, 0)),
            pl.BlockSpec((1, D_MODEL), lambda i: (0, 0)),
        ],
        out_specs=pl.BlockSpec((TM_PROJ, D_MODEL), lambda i: (i, 0)),
        compiler_params=pltpu.CompilerParams(dimension_semantics=("arbitrary",)),
        name="final_norm",
    )(x, norm_w)


def kernel(x, w_in, lower_bounds, hgrn_norm_w, conv_w, w_proj_hgrn, w_proj_conv, w_out,
           norm_mix, norm_ffn, dense_w1, dense_w3, dense_w2, router_w,
           expert_w1, expert_w3, expert_w2, final_norm):
    batch, seq, d = x.shape
    depth = w_in.shape[0]
    assert d == D_MODEL and seq % CT_REC == 0 and (batch * seq) % TM_PROJ == 0
    assert w_in.shape[2] == N_GROUPS * D_MODEL

    xt = x.reshape(batch * seq, d)
    tri = jnp.tril(jnp.ones((CHUNK, CHUNK), F32))
    lower_bounds = lower_bounds.astype(F32)
    final_w = final_norm.reshape(1, d)
    normed = False

    for layer in range(depth):
        proj = _inproj(xt, norm_mix[layer].reshape(1, d), w_in[layer].astype(BF16))
        o_gated = _recurrence(proj, lower_bounds, hgrn_norm_w[layer].reshape(1, HEAD_DIM),
                              tri, layer, batch, seq)
        xt = _mixout(xt, o_gated, proj, conv_w[layer],
                     w_proj_hgrn[layer].astype(BF16), w_proj_conv[layer].astype(BF16),
                     w_out[layer].astype(BF16), seq)
        j = layer // 2
        nw = norm_ffn[layer].reshape(1, d)
        if layer % 2 == 0:
            xt = _dense_ffn(xt, nw, dense_w1[j].astype(BF16), dense_w3[j].astype(BF16),
                            dense_w2[j].astype(BF16))
        else:
            normed = layer == depth - 1
            xt = _moe_ffn(xt, nw, router_w[j].astype(F32), expert_w1[j].astype(BF16),
                          expert_w3[j].astype(BF16), expert_w2[j].astype(BF16),
                          final_w, normed)
    out = xt if normed else _final_norm(xt, final_w)
    return out.reshape(batch, seq, d)
```

```python
import functools

import jax
import jax.numpy as jnp
from jax import lax
from jax.experimental import pallas as pl
from jax.experimental.pallas import tpu as pltpu

D_MODEL = 1024
HEADS = 8
HEAD_DIM = 128
HGRN_SCALE = HEAD_DIM ** -0.5
F_MIN = 1e-6
CONV_K = 3
D_FF = 2816
N_EXPERTS = 8
TOP_K = 2
EPS = 1e-6
N_GROUPS = 9
G_Q, G_F, G_I, G_G, G_B, G_C, G_U, G_GA, G_GB = range(N_GROUPS)

LANES = 128
SUBLANES = 8
BF16_SUBLANES = 16
VMEM_LIMIT = 56 * 1024 * 1024

TM_PROJ = 1024
TN_PROJ = 2304
TM_MIX = 512
TM_FFN = 512
FF_CHUNK = D_FF // 2
TM_ROUTE = 512
TM_DISP = 512
TM_GRP = 512
TM_COMB = 256
DMA_ISSUE_UNROLL = 8
CT_REC = 1024
CHUNK = 128
SUB = 32
N_SUB = CHUNK // SUB
REC_UNROLL = 2

F32 = jnp.float32
BF16 = jnp.bfloat16
I32 = jnp.int32


def _rms_norm_rows(x, w):
    ms = jnp.mean(x * x, axis=-1, keepdims=True)
    return x * lax.rsqrt(ms + EPS) * w


def _sigmoid(x):
    return 1.0 / (1.0 + jnp.exp(-x))


def _silu(x):
    return x * _sigmoid(x)


def _split_bf16(v):
    hi = v.astype(BF16)
    lo = (v - hi.astype(F32)).astype(BF16)
    return hi, lo


def _dot3(a_hi, a_lo, b_hi, b_lo, dims):
    dg = functools.partial(lax.dot_general, dimension_numbers=dims,
                           preferred_element_type=F32)
    return dg(a_hi, b_hi) + dg(a_hi, b_lo) + dg(a_lo, b_hi)


def _inproj_kernel(x_ref, nw_ref, w_ref, o_ref, h_scr):
    @pl.when(pl.program_id(1) == 0)
    def _():
        h_scr[...] = _rms_norm_rows(x_ref[...], nw_ref[...]).astype(BF16)

    o_ref[...] = jnp.dot(h_scr[...], w_ref[...],
                         preferred_element_type=F32).astype(o_ref.dtype)


def _inproj(x, norm_w, w_in_bf16):
    t = x.shape[0]
    n = w_in_bf16.shape[1]
    return pl.pallas_call(
        _inproj_kernel,
        out_shape=jax.ShapeDtypeStruct((t, n), BF16),
        grid=(t // TM_PROJ, n // TN_PROJ),
        in_specs=[
            pl.BlockSpec((TM_PROJ, D_MODEL), lambda i, j: (i, 0)),
            pl.BlockSpec((1, D_MODEL), lambda i, j: (0, 0)),
            pl.BlockSpec((D_MODEL, TN_PROJ), lambda i, j: (0, j)),
        ],
        out_specs=pl.BlockSpec((TM_PROJ, TN_PROJ), lambda i, j: (i, j)),
        scratch_shapes=[pltpu.VMEM((TM_PROJ, D_MODEL), BF16)],
        compiler_params=pltpu.CompilerParams(
            dimension_semantics=("arbitrary", "arbitrary"),
            vmem_limit_bytes=VMEM_LIMIT),
        name="inproj",
    )(x, norm_w, w_in_bf16)


def _layer_lower_bound(lb_raw, layer):
    m = jnp.max(lb_raw, axis=0, keepdims=True)
    e = jnp.exp(lb_raw - m)
    soft = e / jnp.sum(e, axis=0, keepdims=True)
    acc = soft[0:1, :]
    for j in range(1, layer + 1):
        acc = acc + soft[j:j + 1, :]
    return acc - soft[0:1, :]


def _rec_chunk_head(q_raw, z, v_bf, g_raw, lb, gw, st, tri_f32, tri_bf):
    qh = _silu(q_raw)
    e = jnp.exp(-jnp.abs(z))
    r = 1.0 / (1.0 + e)
    er = e * r
    pos = z >= 0.0
    sig_pos = jnp.where(pos, r, er)
    sig_neg = jnp.where(pos, er, r)
    one_m_lb = 1.0 - lb
    f = lb + one_m_lb * sig_pos
    logf = jnp.log2(jnp.maximum(f, F_MIN))
    kk = one_m_lb * sig_neg

    hi, lo = _split_bf16(logf)
    cum2 = jnp.dot(tri_bf, jnp.concatenate([hi, lo], axis=1),
                   preferred_element_type=F32)
    cum = cum2[:, :HEAD_DIM] + cum2[:, HEAD_DIM:]

    s_ref = [jnp.zeros((1, HEAD_DIM), F32)]
    for j in range(1, N_SUB):
        s_ref.append(cum[j * SUB - 1:j * SUB, :])
    cum_end = cum[CHUNK - 1:CHUNK, :]

    q_loc, k_loc = [], []
    for j in range(N_SUB):
        rows = slice(j * SUB, (j + 1) * SUB)
        q_loc.append(qh[rows] * jnp.exp2(cum[rows] - s_ref[j]))
        k_loc.append(kk[rows] * jnp.exp2(s_ref[j] - cum[rows]))

    zeros_blk = jnp.zeros((SUB, HEAD_DIM), BF16)
    lhs_segs, rhs_segs = [], []
    for j in range(N_SUB):
        lhs_rows, rhs_rows = [], []
        for i in range(N_SUB):
            if i < j:
                lhs_rows.append(zeros_blk)
            elif i == j:
                lhs_rows.append(q_loc[i].astype(BF16))
            else:
                lhs_rows.append((q_loc[i] * jnp.exp2(s_ref[i] - s_ref[j])).astype(BF16))
            rhs_rows.append(k_loc[i].astype(BF16) if i == j else zeros_blk)
        lhs_segs.append(jnp.concatenate(lhs_rows, axis=0))
        rhs_segs.append(jnp.concatenate(rhs_rows, axis=0))
    lhs = jnp.concatenate(lhs_segs, axis=1)
    rhs = jnp.concatenate(rhs_segs, axis=1)
    scores = lax.dot_general(lhs, rhs, (((1,), (1,)), ((), ())),
                             preferred_element_type=F32) * tri_f32
    q_inter = lhs_segs[0]
    o = (jnp.dot(scores.astype(BF16), v_bf, preferred_element_type=F32)
         + lax.dot_general(q_inter, st.astype(BF16), (((1,), (1,)), ((), ())),
                           preferred_element_type=F32))

    k_state = jnp.concatenate(
        [(k_loc[j] * jnp.exp2(cum_end - s_ref[j])).astype(BF16) for j in range(N_SUB)],
        axis=0)
    st_new = st * jnp.exp2(cum_end) + lax.dot_general(
        v_bf, k_state, (((0,), (0,)), ((), ())), preferred_element_type=F32)

    ms = jnp.mean(o * o, axis=-1, keepdims=True)
    out = o * lax.rsqrt(ms + EPS / (HGRN_SCALE * HGRN_SCALE)) * gw * _silu(g_raw)
    return out, st_new


def _rec_kernel(layer, q_ref, f_ref, i_ref, g_ref, lbraw_ref, gw_ref, tri_ref,
                o_ref, st_ref):
    @pl.when(pl.program_id(1) == 0)
    def _():
        st_ref[...] = jnp.zeros_like(st_ref)

    lb_all = _layer_lower_bound(lbraw_ref[...], layer)
    gw = gw_ref[...]
    tri_f32 = tri_ref[...]
    tri_bf = tri_f32.astype(BF16)

    def chunk_body(c, carry):
        r0 = pl.multiple_of(c * CHUNK, CHUNK)
        rows = pl.ds(r0, CHUNK)
        for h in range(HEADS):
            cols = slice(h * HEAD_DIM, (h + 1) * HEAD_DIM)
            out, st_new = _rec_chunk_head(
                q_ref[rows, cols].astype(F32), f_ref[rows, cols].astype(F32),
                i_ref[rows, cols], g_ref[rows, cols].astype(F32),
                lb_all[:, cols], gw, st_ref[h], tri_f32, tri_bf)
            st_ref[h] = st_new
            o_ref[rows, cols] = out.astype(o_ref.dtype)
        return carry

    lax.fori_loop(0, CT_REC // CHUNK, chunk_body, 0, unroll=REC_UNROLL)


def _recurrence(proj, lower_bounds, gnorm_w, tri, layer, batch, seq):
    t = proj.shape[0]
    steps = seq // CT_REC

    def group_spec(g):
        return pl.BlockSpec((CT_REC, D_MODEL), lambda b, s: (b * steps + s, g))

    return pl.pallas_call(
        functools.partial(_rec_kernel, layer),
        out_shape=jax.ShapeDtypeStruct((t, D_MODEL), BF16),
        grid=(batch, steps),
        in_specs=[
            group_spec(G_Q), group_spec(G_F), group_spec(G_I), group_spec(G_G),
            pl.BlockSpec(lower_bounds.shape, lambda b, s: (0, 0)),
            pl.BlockSpec((1, HEAD_DIM), lambda b, s: (0, 0)),
            pl.BlockSpec((CHUNK, CHUNK), lambda b, s: (0, 0)),
        ],
        out_specs=pl.BlockSpec((CT_REC, D_MODEL), lambda b, s: (b * steps + s, 0)),
        scratch_shapes=[pltpu.VMEM((HEADS, HEAD_DIM, HEAD_DIM), F32)],
        compiler_params=pltpu.CompilerParams(
            dimension_semantics=("arbitrary", "arbitrary"),
            vmem_limit_bytes=VMEM_LIMIT),
        name="hgrn2_recurrence",
    )(proj, proj, proj, proj, lower_bounds, gnorm_w, tri)


def _mixout_kernel(seq, x_ref, oa_ref, b_ref, c_ref, u_ref, ga_ref, gb_ref,
                   bh_ref, uh_ref, cw_ref, wph_ref, wpc_ref, wo_ref, o_ref):
    i = pl.program_id(0)
    v = b_ref[...].astype(F32) * u_ref[...].astype(F32)
    at_seq_start = (i * TM_MIX) % seq == 0
    halo = bh_ref[...].astype(F32) * uh_ref[...].astype(F32)
    halo = jnp.where(at_seq_start, 0.0, halo)
    prev1 = halo[BF16_SUBLANES - 1:BF16_SUBLANES, :]
    prev2 = halo[BF16_SUBLANES - 2:BF16_SUBLANES - 1, :]
    row = lax.broadcasted_iota(I32, v.shape, 0)
    v1 = jnp.where(row == 0, prev1, pltpu.roll(v, 1, 0))
    v2 = jnp.where(row == 0, prev2,
                   jnp.where(row == 1, prev1, pltpu.roll(v, 2, 0)))
    cw = cw_ref[...]
    conv = cw[0:1, :] * v2 + cw[1:2, :] * v1 + cw[2:3, :] * v
    yb_in = (c_ref[...].astype(F32) * conv).astype(BF16)
    y_b = jnp.dot(yb_in, wpc_ref[...], preferred_element_type=F32)
    y_a = jnp.dot(oa_ref[...], wph_ref[...], preferred_element_type=F32)
    merged = (_sigmoid(ga_ref[...].astype(F32)) * y_a
              + _sigmoid(gb_ref[...].astype(F32)) * y_b).astype(BF16)
    o_ref[...] = x_ref[...] + jnp.dot(merged, wo_ref[...], preferred_element_type=F32)


def _mixout(x, o_gated, proj, conv_w, wph, wpc, wo, seq):
    t = x.shape[0]
    halo_per_tile = TM_MIX // BF16_SUBLANES

    def group_spec(g):
        return pl.BlockSpec((TM_MIX, D_MODEL), lambda i: (i, g))

    def halo_spec(g):
        return pl.BlockSpec(
            (BF16_SUBLANES, D_MODEL),
            lambda i: (jnp.maximum(i * halo_per_tile - 1, 0), g))

    def weight_spec():
        return pl.BlockSpec((D_MODEL, D_MODEL), lambda i: (0, 0))

    return pl.pallas_call(
        functools.partial(_mixout_kernel, seq),
        out_shape=jax.ShapeDtypeStruct((t, D_MODEL), F32),
        grid=(t // TM_MIX,),
        in_specs=[
            pl.BlockSpec((TM_MIX, D_MODEL), lambda i: (i, 0)),
            pl.BlockSpec((TM_MIX, D_MODEL), lambda i: (i, 0)),
            group_spec(G_B), group_spec(G_C), group_spec(G_U),
            group_spec(G_GA), group_spec(G_GB),
            halo_spec(G_B), halo_spec(G_U),
            pl.BlockSpec((CONV_K, D_MODEL), lambda i: (0, 0)),
            weight_spec(), weight_spec(), weight_spec(),
        ],
        out_specs=pl.BlockSpec((TM_MIX, D_MODEL), lambda i: (i, 0)),
        compiler_params=pltpu.CompilerParams(
            dimension_semantics=("arbitrary",),
            vmem_limit_bytes=VMEM_LIMIT),
        name="mixer_out",
    )(x, o_gated, proj, proj, proj, proj, proj, proj, proj, conv_w, wph, wpc, wo)


def _dense_ffn_kernel(x_ref, nw_ref, w1_ref, w3_ref, w2_ref, o_ref):
    x = x_ref[...]
    h = _rms_norm_rows(x, nw_ref[...]).astype(BF16)
    a = jnp.dot(h, w1_ref[...], preferred_element_type=F32)
    b = jnp.dot(h, w3_ref[...], preferred_element_type=F32)
    act = (_silu(a) * b).astype(BF16)
    o_ref[...] = x + jnp.dot(act, w2_ref[...], preferred_element_type=F32)


def _dense_ffn(x, norm_w, w1, w3, w2):
    t = x.shape[0]
    return pl.pallas_call(
        _dense_ffn_kernel,
        out_shape=jax.ShapeDtypeStruct((t, D_MODEL), F32),
        grid=(t // TM_FFN,),
        in_specs=[
            pl.BlockSpec((TM_FFN, D_MODEL), lambda i: (i, 0)),
            pl.BlockSpec((1, D_MODEL), lambda i: (0, 0)),
            pl.BlockSpec((D_MODEL, D_FF), lambda i: (0, 0)),
            pl.BlockSpec((D_MODEL, D_FF), lambda i: (0, 0)),
            pl.BlockSpec((D_FF, D_MODEL), lambda i: (0, 0)),
        ],
        out_specs=pl.BlockSpec((TM_FFN, D_MODEL), lambda i: (i, 0)),
        compiler_params=pltpu.CompilerParams(
            dimension_semantics=("arbitrary",),
            vmem_limit_bytes=VMEM_LIMIT),
        name="dense_ffn",
    )(x, norm_w, w1, w3, w2)


def _route_kernel(x_ref, nw_ref, rwt_ref, rwp_ref, triu_ref,
                  h_ref, meta_ref, wts_ref, cnt_ref, carry_scr):
    @pl.when(pl.program_id(0) == 0)
    def _():
        carry_scr[...] = jnp.zeros_like(carry_scr)

    h = _rms_norm_rows(x_ref[...], nw_ref[...])
    h_ref[...] = h
    h_hi, h_lo = _split_bf16(h)

    rwt_hi, rwt_lo = _split_bf16(rwt_ref[...])
    lt = _dot3(rwt_hi, rwt_lo, h_hi, h_lo, (((1,), (1,)), ((), ())))
    sub = lax.broadcasted_iota(I32, lt.shape, 0)
    m1 = jnp.max(lt, axis=0, keepdims=True)
    i1 = jnp.min(jnp.where(lt == m1, sub, N_EXPERTS), axis=0, keepdims=True)
    rest = jnp.where(sub == i1, -jnp.inf, lt)
    m2 = jnp.max(rest, axis=0, keepdims=True)
    i2 = jnp.min(jnp.where(rest == m2, sub, N_EXPERTS), axis=0, keepdims=True)
    oh1 = sub == i1
    oh2 = sub == i2
    member = jnp.where(jnp.logical_or(oh1, oh2), 1.0, 0.0)
    prefix = jnp.dot(member.astype(BF16), triu_ref[...], preferred_element_type=F32)
    carry = carry_scr[...]
    base = carry[:, 0:1] + prefix
    rank1 = jnp.sum(jnp.where(oh1, base, 0.0), axis=0, keepdims=True)
    rank2 = jnp.sum(jnp.where(oh2, base, 0.0), axis=0, keepdims=True)
    carry_new = carry + jnp.sum(member, axis=1, keepdims=True)
    carry_scr[...] = carry_new
    cnt_ref[...] = carry_new.astype(I32)
    meta_ref[...] = jnp.where(
        sub == 0, i1, jnp.where(
            sub == 1, i2, jnp.where(
                sub == 2, rank1.astype(I32), jnp.where(
                    sub == 3, rank2.astype(I32), 0))))

    rwp_hi, rwp_lo = _split_bf16(rwp_ref[...])
    lr = _dot3(h_hi, h_lo, rwp_hi, rwp_lo, (((1,), (0,)), ((), ())))
    lane = lax.broadcasted_iota(I32, lr.shape, 1)
    lr = jnp.where(lane < N_EXPERTS, lr, -jnp.inf)
    m1r = jnp.max(lr, axis=-1, keepdims=True)
    i1r = jnp.min(jnp.where(lr == m1r, lane, LANES), axis=-1, keepdims=True)
    m2r = jnp.max(jnp.where(lane == i1r, -jnp.inf, lr), axis=-1, keepdims=True)
    e2 = jnp.exp(m2r - m1r)
    w1 = 1.0 / (1.0 + e2)
    w2 = e2 * w1
    wts_ref[...] = jnp.where(lane == 0, w1, jnp.where(lane == 1, w2, 0.0))


def _route(x, norm_w, router_w):
    t = x.shape[0]
    rwt = router_w.T
    rwp = jnp.pad(router_w, ((0, 0), (0, LANES - N_EXPERTS)))
    triu = jnp.triu(jnp.ones((TM_ROUTE, TM_ROUTE), BF16), k=1)
    return pl.pallas_call(
        _route_kernel,
        out_shape=(
            jax.ShapeDtypeStruct((t, D_MODEL), F32),
            jax.ShapeDtypeStruct((SUBLANES, t), I32),
            jax.ShapeDtypeStruct((t, LANES), F32),
            jax.ShapeDtypeStruct((N_EXPERTS, LANES), I32),
        ),
        grid=(t // TM_ROUTE,),
        in_specs=[
            pl.BlockSpec((TM_ROUTE, D_MODEL), lambda i: (i, 0)),
            pl.BlockSpec((1, D_MODEL), lambda i: (0, 0)),
            pl.BlockSpec((N_EXPERTS, D_MODEL), lambda i: (0, 0)),
            pl.BlockSpec((D_MODEL, LANES), lambda i: (0, 0)),
            pl.BlockSpec((TM_ROUTE, TM_ROUTE), lambda i: (0, 0)),
        ],
        out_specs=(
            pl.BlockSpec((TM_ROUTE, D_MODEL), lambda i: (i, 0)),
            pl.BlockSpec((SUBLANES, TM_ROUTE), lambda i: (0, i)),
            pl.BlockSpec((TM_ROUTE, LANES), lambda i: (i, 0)),
            pl.BlockSpec((N_EXPERTS, LANES), lambda i: (0, 0)),
        ),
        scratch_shapes=[pltpu.VMEM((N_EXPERTS, LANES), F32)],
        compiler_params=pltpu.CompilerParams(
            dimension_semantics=("arbitrary",),
            vmem_limit_bytes=VMEM_LIMIT),
        name="moe_route",
    )(x, norm_w, rwt, rwp, triu)


def _pos_copy(pos_hbm, pos_smem, sem, step, slot):
    return pltpu.make_async_copy(pos_hbm.at[step], pos_smem.at[slot], sem.at[slot])


def _dispatch_kernel(fill_ref, pos_hbm, h_ref, xs_hbm, pos_smem, hbuf, zbuf, psem, dsem, zsem):
    i = pl.program_id(0)
    n = pl.num_programs(0)
    slot = i % 2

    @pl.when(i == 0)
    def _():
        _pos_copy(pos_hbm, pos_smem, psem, 0, 0).start()

    _pos_copy(pos_hbm, pos_smem, psem, i, slot).wait()

    @pl.when(i + 1 < n)
    def _():
        _pos_copy(pos_hbm, pos_smem, psem, i + 1, 1 - slot).start()

    def wait_rows_from(s):
        for _ in range(TOP_K):
            pltpu.make_async_copy(hbuf.at[s], xs_hbm.at[pl.ds(0, TM_DISP)],
                                  dsem.at[s]).wait()

    @pl.when(i >= 2)
    def _():
        wait_rows_from(slot)

    hbuf[slot] = h_ref[...]

    def issue(t, carry):
        for j in range(TOP_K):
            p = pos_smem[slot, j * TM_DISP + t]
            pltpu.make_async_copy(hbuf.at[slot, pl.ds(t, 1)],
                                  xs_hbm.at[pl.ds(p, 1)], dsem.at[slot]).start()
        return carry

    lax.fori_loop(0, TM_DISP, issue, 0, unroll=DMA_ISSUE_UNROLL)

    @pl.when(i == n - 1)
    def _():
        wait_rows_from(slot)

        @pl.when(n > 1)
        def _():
            wait_rows_from(1 - slot)

        zbuf[...] = jnp.zeros_like(zbuf)

        def zero_row(r, carry):
            cp = pltpu.make_async_copy(zbuf.at[pl.ds(0, 1)], xs_hbm.at[pl.ds(r, 1)], zsem)
            cp.start()
            cp.wait()
            return carry

        def zero_tile(k, carry):
            r0 = pl.multiple_of(k * TM_GRP, TM_GRP)
            cp = pltpu.make_async_copy(zbuf, xs_hbm.at[pl.ds(r0, TM_GRP)], zsem)
            cp.start()
            cp.wait()
            return carry

        for e in range(N_EXPERTS):
            lax.fori_loop(fill_ref[e], fill_ref[N_EXPERTS + e], zero_row, 0)
        lax.fori_loop(fill_ref[2 * N_EXPERTS], xs_hbm.shape[0] // TM_GRP, zero_tile, 0)


def _dispatch(h, pos_tiles, fill_bounds, n_sorted_rows):
    t = h.shape[0]
    return pl.pallas_call(
        _dispatch_kernel,
        out_shape=jax.ShapeDtypeStruct((n_sorted_rows, D_MODEL), F32),
        grid_spec=pltpu.PrefetchScalarGridSpec(
            num_scalar_prefetch=1,
            grid=(t // TM_DISP,),
            in_specs=[pl.BlockSpec(memory_space=pl.ANY),
                      pl.BlockSpec((TM_DISP, D_MODEL), lambda i, fill: (i, 0))],
            out_specs=pl.BlockSpec(memory_space=pl.ANY),
            scratch_shapes=[
                pltpu.SMEM((2, TOP_K * TM_DISP), I32),
                pltpu.VMEM((2, TM_DISP, D_MODEL), F32),
                pltpu.VMEM((TM_GRP, D_MODEL), F32),
                pltpu.SemaphoreType.DMA((2,)),
                pltpu.SemaphoreType.DMA((2,)),
                pltpu.SemaphoreType.DMA(()),
            ]),
        compiler_params=pltpu.CompilerParams(
            dimension_semantics=("arbitrary",),
            vmem_limit_bytes=VMEM_LIMIT),
        name="moe_dispatch",
    )(fill_bounds, pos_tiles, h)


def _experts_kernel(te_ref, nv_ref, xs_ref, w1_ref, w3_ref, w2_ref, y_ref, xb_scr):
    k = pl.program_id(0)
    fc = pl.program_id(1)

    @pl.when(k < nv_ref[0])
    def _():
        @pl.when(fc == 0)
        def _():
            xb_scr[...] = xs_ref[...].astype(BF16)

        xb = xb_scr[...]
        a = jnp.dot(xb, w1_ref[0], preferred_element_type=F32)
        b = jnp.dot(xb, w3_ref[0], preferred_element_type=F32)
        act = (_silu(a) * b).astype(BF16)
        y = jnp.dot(act, w2_ref[0], preferred_element_type=F32)

        @pl.when(fc == 0)
        def _():
            y_ref[...] = y

        @pl.when(fc > 0)
        def _():
            y_ref[...] += y

    @pl.when(jnp.logical_and(k >= nv_ref[0], fc == 0))
    def _():
        y_ref[...] = jnp.zeros_like(y_ref)


def _experts(xs, tile_expert, n_valid, w1, w3, w2):
    n_rows = xs.shape[0]
    n_fc = D_FF // FF_CHUNK

    def row_map(k, f, te, nv):
        return (jnp.minimum(k, nv[0] - 1), 0)

    def out_map(k, f, te, nv):
        return (k, 0)

    def fc_of(k, f, nv):
        return jnp.where(k < nv[0], f, n_fc - 1)

    def w13_map(k, f, te, nv):
        return (te[jnp.minimum(k, nv[0] - 1)], 0, fc_of(k, f, nv))

    def w2_map(k, f, te, nv):
        return (te[jnp.minimum(k, nv[0] - 1)], fc_of(k, f, nv), 0)

    return pl.pallas_call(
        _experts_kernel,
        out_shape=jax.ShapeDtypeStruct((n_rows, D_MODEL), F32),
        grid_spec=pltpu.PrefetchScalarGridSpec(
            num_scalar_prefetch=2,
            grid=(n_rows // TM_GRP, n_fc),
            in_specs=[
                pl.BlockSpec((TM_GRP, D_MODEL), row_map),
                pl.BlockSpec((1, D_MODEL, FF_CHUNK), w13_map),
                pl.BlockSpec((1, D_MODEL, FF_CHUNK), w13_map),
                pl.BlockSpec((1, FF_CHUNK, D_MODEL), w2_map),
            ],
            out_specs=pl.BlockSpec((TM_GRP, D_MODEL), out_map),
            scratch_shapes=[pltpu.VMEM((TM_GRP, D_MODEL), BF16)]),
        compiler_params=pltpu.CompilerParams(
            dimension_semantics=("arbitrary", "arbitrary"),
            vmem_limit_bytes=VMEM_LIMIT),
        name="moe_experts",
    )(tile_expert, n_valid, xs, w1, w3, w2)


def _combine_kernel(fuse_norm, pos_hbm, y_hbm, x_ref, wts_ref, fnw_ref, o_ref,
                    pos_smem, ybuf, psem, gsem):
    i = pl.program_id(0)
    n = pl.num_programs(0)
    slot = i % 2

    def issue_gathers(s):
        def body(t, carry):
            for j in range(TOP_K):
                p = pos_smem[s, j * TM_COMB + t]
                pltpu.make_async_copy(y_hbm.at[pl.ds(p, 1)],
                                      ybuf.at[s, j, pl.ds(t, 1)], gsem.at[s]).start()
            return carry

        lax.fori_loop(0, TM_COMB, body, 0, unroll=DMA_ISSUE_UNROLL)

    @pl.when(i == 0)
    def _():
        first = _pos_copy(pos_hbm, pos_smem, psem, 0, 0)
        first.start()
        first.wait()
        issue_gathers(0)

        @pl.when(n > 1)
        def _():
            _pos_copy(pos_hbm, pos_smem, psem, 1, 1).start()

    @pl.when(i + 1 < n)
    def _():
        _pos_copy(pos_hbm, pos_smem, psem, i + 1, 1 - slot).wait()
        issue_gathers(1 - slot)

        @pl.when(i + 2 < n)
        def _():
            _pos_copy(pos_hbm, pos_smem, psem, i + 2, slot).start()

    for j in range(TOP_K):
        pltpu.make_async_copy(y_hbm.at[pl.ds(0, TM_COMB)], ybuf.at[slot, j],
                              gsem.at[slot]).wait()

    w = wts_ref[...]
    out = x_ref[...] + w[:, 0:1] * ybuf[slot, 0] + w[:, 1:2] * ybuf[slot, 1]
    if fuse_norm:
        out = _rms_norm_rows(out, fnw_ref[...])
    o_ref[...] = out


def _combine(x, y, pos_tiles, wts, final_norm_w, fuse_norm):
    t = x.shape[0]
    return pl.pallas_call(
        functools.partial(_combine_kernel, fuse_norm),
        out_shape=jax.ShapeDtypeStruct((t, D_MODEL), F32),
        grid=(t // TM_COMB,),
        in_specs=[
            pl.BlockSpec(memory_space=pl.ANY),
            pl.BlockSpec(memory_space=pl.ANY),
            pl.BlockSpec((TM_COMB, D_MODEL), lambda i: (i, 0)),
            pl.BlockSpec((TM_COMB, LANES), lambda i: (i, 0)),
            pl.BlockSpec((1, D_MODEL), lambda i: (0, 0)),
        ],
        out_specs=pl.BlockSpec((TM_COMB, D_MODEL), lambda i: (i, 0)),
        scratch_shapes=[
            pltpu.SMEM((2, TOP_K * TM_COMB), I32),
            pltpu.VMEM((2, TOP_K, TM_COMB, D_MODEL), F32),
            pltpu.SemaphoreType.DMA((2,)),
            pltpu.SemaphoreType.DMA((2,)),
        ],
        compiler_params=pltpu.CompilerParams(dimension_semantics=("arbitrary",)),
        name="moe_combine",
    )(pos_tiles, y, x, wts, final_norm_w)


def _tile_major(pos, tile):
    t = pos.shape[1]
    return pos.reshape(TOP_K, t // tile, tile).transpose(1, 0, 2).reshape(t // tile, TOP_K * tile)


def _moe_ffn(x, norm_w, router_w, w1, w3, w2, final_norm_w, fuse_norm):
    t = x.shape[0]
    h, meta, wts, cnt = _route(x, norm_w, router_w)

    counts = cnt[:, 0]
    padded = ((counts + TM_GRP - 1) // TM_GRP) * TM_GRP
    ends = jnp.cumsum(padded)
    offs = ends - padded
    n_sorted_rows = TOP_K * t + N_EXPERTS * TM_GRP
    tile_start = jnp.arange(n_sorted_rows // TM_GRP, dtype=I32) * TM_GRP
    tile_expert = jnp.minimum(
        jnp.sum((tile_start[:, None] >= ends[None, :]).astype(I32), axis=1), N_EXPERTS - 1)
    n_valid = (ends[-1] // TM_GRP).astype(I32).reshape(1)
    pos = jnp.stack([offs[meta[0]] + meta[2], offs[meta[1]] + meta[3]]).astype(I32)
    fill_bounds = jnp.concatenate([offs + counts, ends, n_valid]).astype(I32)

    xs = _dispatch(h, _tile_major(pos, TM_DISP), fill_bounds, n_sorted_rows)
    y = _experts(xs, tile_expert.astype(I32), n_valid, w1, w3, w2)
    return _combine(x, y, _tile_major(pos, TM_COMB), wts, final_norm_w, fuse_norm)


def _final_norm_kernel(x_ref, nw_ref, o_ref):
    o_ref[...] = _rms_norm_rows(x_ref[...], nw_ref[...])


def _final_norm(x, norm_w):
    t = x.shape[0]
    return pl.pallas_call(
        _final_norm_kernel,
        out_shape=jax.ShapeDtypeStruct((t, D_MODEL), F32),
        grid=(t // TM_PROJ,),
        in_specs=[
            pl.BlockSpec((TM_PROJ, D_MODEL), lambda i: (i, 0)),
            pl.BlockSpec((1, D_MODEL), lambda i: (0, 0)),
        ],
        out_specs=pl.BlockSpec((TM_PROJ, D_MODEL), lambda i: (i, 0)),
        compiler_params=pltpu.CompilerParams(dimension_semantics=("arbitrary",)),
        name="final_norm",
    )(x, norm_w)


def kernel(x, w_in, lower_bounds, hgrn_norm_w, conv_w, w_proj_hgrn, w_proj_conv, w_out,
           norm_mix, norm_ffn, dense_w1, dense_w3, dense_w2, router_w,
           expert_w1, expert_w3, expert_w2, final_norm):
    batch, seq, d = x.shape
    depth = w_in.shape[0]
    assert d == D_MODEL and seq % CT_REC == 0 and (batch * seq) % TM_PROJ == 0
    assert w_in.shape[2] == N_GROUPS * D_MODEL

    xt = x.reshape(batch * seq, d)
    tri = jnp.tril(jnp.ones((CHUNK, CHUNK), F32))
    lower_bounds = lower_bounds.astype(F32)
    final_w = final_norm.reshape(1, d)
    normed = False

    for layer in range(depth):
        proj = _inproj(xt, norm_mix[layer].reshape(1, d), w_in[layer].astype(BF16))
        o_gated = _recurrence(proj, lower_bounds, hgrn_norm_w[layer].reshape(1, HEAD_DIM),
                              tri, layer, batch, seq)
        xt = _mixout(xt, o_gated, proj, conv_w[layer],
                     w_proj_hgrn[layer].astype(BF16), w_proj_conv[layer].astype(BF16),
                     w_out[layer].astype(BF16), seq)
        j = layer // 2
        nw = norm_ffn[layer].reshape(1, d)
        if layer % 2 == 0:
            xt = _dense_ffn(xt, nw, dense_w1[j].astype(BF16), dense_w3[j].astype(BF16),
                            dense_w2[j].astype(BF16))
        else:
            normed = layer == depth - 1
            xt = _moe_ffn(xt, nw, router_w[j].astype(F32), expert_w1[j].astype(BF16),
                          expert_w3[j].astype(BF16), expert_w2[j].astype(BF16),
                          final_w, normed)
    out = xt if normed else _final_norm(xt, final_w)
    return out.reshape(batch, seq, d)
```

```python
import functools

import jax
import jax.numpy as jnp
from jax import lax
from jax.experimental import pallas as pl
from jax.experimental.pallas import tpu as pltpu

D_MODEL = 1024
HEADS = 8
HEAD_DIM = 128
HGRN_SCALE = HEAD_DIM ** -0.5
F_MIN = 1e-6
CONV_K = 3
D_FF = 2816
N_EXPERTS = 8
TOP_K = 2
EPS = 1e-6
N_GROUPS = 9
G_Q, G_F, G_I, G_G, G_B, G_C, G_U, G_GA, G_GB = range(N_GROUPS)

LANES = 128
SUBLANES = 8
ROW_TILE = SUBLANES
BF16_SUBLANES = 16
VMEM_LIMIT = 56 * 1024 * 1024

TM_PROJ = 1024
TN_PROJ = 2304
TM_MIX = 512
TM_FFN = 512
FF_CHUNK = D_FF // 2
TM_ROUTE = 512
TM_DISP = 512
TM_GRP = 512
TM_COMB = 512
DMA_ISSUE_UNROLL = 8
CT_REC = 1024
CHUNK = 128
SUB = 32
N_SUB = CHUNK // SUB
REC_UNROLL = 2

F32 = jnp.float32
BF16 = jnp.bfloat16
I32 = jnp.int32


def _rms_norm_rows(x, w):
    ms = jnp.mean(x * x, axis=-1, keepdims=True)
    return x * lax.rsqrt(ms + EPS) * w


def _sigmoid(x):
    return 1.0 / (1.0 + jnp.exp(-x))


def _silu(x):
    return x * _sigmoid(x)


def _split_bf16(v):
    hi = v.astype(BF16)
    lo = (v - hi.astype(F32)).astype(BF16)
    return hi, lo


def _dot3(a_hi, a_lo, b_hi, b_lo, dims):
    dg = functools.partial(lax.dot_general, dimension_numbers=dims,
                           preferred_element_type=F32)
    return dg(a_hi, b_hi) + dg(a_hi, b_lo) + dg(a_lo, b_hi)


def _inproj_kernel(x_ref, nw_ref, w_ref, o_ref, h_scr):
    @pl.when(pl.program_id(1) == 0)
    def _():
        h_scr[...] = _rms_norm_rows(x_ref[...], nw_ref[...]).astype(BF16)

    o_ref[...] = jnp.dot(h_scr[...], w_ref[...],
                         preferred_element_type=F32).astype(o_ref.dtype)


def _inproj(x, norm_w, w_in_bf16):
    t = x.shape[0]
    n = w_in_bf16.shape[1]
    return pl.pallas_call(
        _inproj_kernel,
        out_shape=jax.ShapeDtypeStruct((t, n), BF16),
        grid=(t // TM_PROJ, n // TN_PROJ),
        in_specs=[
            pl.BlockSpec((TM_PROJ, D_MODEL), lambda i, j: (i, 0)),
            pl.BlockSpec((1, D_MODEL), lambda i, j: (0, 0)),
            pl.BlockSpec((D_MODEL, TN_PROJ), lambda i, j: (0, j)),
        ],
        out_specs=pl.BlockSpec((TM_PROJ, TN_PROJ), lambda i, j: (i, j)),
        scratch_shapes=[pltpu.VMEM((TM_PROJ, D_MODEL), BF16)],
        compiler_params=pltpu.CompilerParams(
            dimension_semantics=("arbitrary", "arbitrary"),
            vmem_limit_bytes=VMEM_LIMIT),
        name="inproj",
    )(x, norm_w, w_in_bf16)


def _layer_lower_bound(lb_raw, layer):
    m = jnp.max(lb_raw, axis=0, keepdims=True)
    e = jnp.exp(lb_raw - m)
    soft = e / jnp.sum(e, axis=0, keepdims=True)
    acc = soft[0:1, :]
    for j in range(1, layer + 1):
        acc = acc + soft[j:j + 1, :]
    return acc - soft[0:1, :]


def _rec_chunk_head(q_raw, z, v_bf, g_raw, lb, gw, st, tri_f32, tri_bf):
    qh = _silu(q_raw)
    e = jnp.exp(-jnp.abs(z))
    r = 1.0 / (1.0 + e)
    er = e * r
    pos = z >= 0.0
    sig_pos = jnp.where(pos, r, er)
    sig_neg = jnp.where(pos, er, r)
    one_m_lb = 1.0 - lb
    f = lb + one_m_lb * sig_pos
    logf = jnp.log2(jnp.maximum(f, F_MIN))
    kk = one_m_lb * sig_neg

    hi, lo = _split_bf16(logf)
    cum2 = jnp.dot(tri_bf, jnp.concatenate([hi, lo], axis=1),
                   preferred_element_type=F32)
    cum = cum2[:, :HEAD_DIM] + cum2[:, HEAD_DIM:]

    s_ref = [jnp.zeros((1, HEAD_DIM), F32)]
    for j in range(1, N_SUB):
        s_ref.append(cum[j * SUB - 1:j * SUB, :])
    cum_end = cum[CHUNK - 1:CHUNK, :]

    q_loc, k_loc = [], []
    for j in range(N_SUB):
        rows = slice(j * SUB, (j + 1) * SUB)
        q_loc.append(qh[rows] * jnp.exp2(cum[rows] - s_ref[j]))
        k_loc.append(kk[rows] * jnp.exp2(s_ref[j] - cum[rows]))

    zeros_blk = jnp.zeros((SUB, HEAD_DIM), BF16)
    lhs_segs, rhs_segs = [], []
    for j in range(N_SUB):
        lhs_rows, rhs_rows = [], []
        for i in range(N_SUB):
            if i < j:
                lhs_rows.append(zeros_blk)
            elif i == j:
                lhs_rows.append(q_loc[i].astype(BF16))
            else:
                lhs_rows.append((q_loc[i] * jnp.exp2(s_ref[i] - s_ref[j])).astype(BF16))
            rhs_rows.append(k_loc[i].astype(BF16) if i == j else zeros_blk)
        lhs_segs.append(jnp.concatenate(lhs_rows, axis=0))
        rhs_segs.append(jnp.concatenate(rhs_rows, axis=0))
    lhs = jnp.concatenate(lhs_segs, axis=1)
    rhs = jnp.concatenate(rhs_segs, axis=1)
    scores = lax.dot_general(lhs, rhs, (((1,), (1,)), ((), ())),
                             preferred_element_type=F32) * tri_f32
    q_inter = lhs_segs[0]
    o = (jnp.dot(scores.astype(BF16), v_bf, preferred_element_type=F32)
         + lax.dot_general(q_inter, st.astype(BF16), (((1,), (1,)), ((), ())),
                           preferred_element_type=F32))

    k_state = jnp.concatenate(
        [(k_loc[j] * jnp.exp2(cum_end - s_ref[j])).astype(BF16) for j in range(N_SUB)],
        axis=0)
    st_new = st * jnp.exp2(cum_end) + lax.dot_general(
        v_bf, k_state, (((0,), (0,)), ((), ())), preferred_element_type=F32)

    ms = jnp.mean(o * o, axis=-1, keepdims=True)
    out = o * lax.rsqrt(ms + EPS / (HGRN_SCALE * HGRN_SCALE)) * gw * _silu(g_raw)
    return out, st_new


def _rec_kernel(layer, q_ref, f_ref, i_ref, g_ref, lbraw_ref, gw_ref, tri_ref,
                o_ref, st_ref):
    @pl.when(pl.program_id(1) == 0)
    def _():
        st_ref[...] = jnp.zeros_like(st_ref)

    lb_all = _layer_lower_bound(lbraw_ref[...], layer)
    gw = gw_ref[...]
    tri_f32 = tri_ref[...]
    tri_bf = tri_f32.astype(BF16)

    def chunk_body(c, carry):
        r0 = pl.multiple_of(c * CHUNK, CHUNK)
        rows = pl.ds(r0, CHUNK)
        for h in range(HEADS):
            cols = slice(h * HEAD_DIM, (h + 1) * HEAD_DIM)
            out, st_new = _rec_chunk_head(
                q_ref[rows, cols].astype(F32), f_ref[rows, cols].astype(F32),
                i_ref[rows, cols], g_ref[rows, cols].astype(F32),
                lb_all[:, cols], gw, st_ref[h], tri_f32, tri_bf)
            st_ref[h] = st_new
            o_ref[rows, cols] = out.astype(o_ref.dtype)
        return carry

    lax.fori_loop(0, CT_REC // CHUNK, chunk_body, 0, unroll=REC_UNROLL)


def _recurrence(proj, lower_bounds, gnorm_w, tri, layer, batch, seq):
    t = proj.shape[0]
    steps = seq // CT_REC

    def group_spec(g):
        return pl.BlockSpec((CT_REC, D_MODEL), lambda b, s: (b * steps + s, g))

    return pl.pallas_call(
        functools.partial(_rec_kernel, layer),
        out_shape=jax.ShapeDtypeStruct((t, D_MODEL), BF16),
        grid=(batch, steps),
        in_specs=[
            group_spec(G_Q), group_spec(G_F), group_spec(G_I), group_spec(G_G),
            pl.BlockSpec(lower_bounds.shape, lambda b, s: (0, 0)),
            pl.BlockSpec((1, HEAD_DIM), lambda b, s: (0, 0)),
            pl.BlockSpec((CHUNK, CHUNK), lambda b, s: (0, 0)),
        ],
        out_specs=pl.BlockSpec((CT_REC, D_MODEL), lambda b, s: (b * steps + s, 0)),
        scratch_shapes=[pltpu.VMEM((HEADS, HEAD_DIM, HEAD_DIM), F32)],
        compiler_params=pltpu.CompilerParams(
            dimension_semantics=("arbitrary", "arbitrary"),
            vmem_limit_bytes=VMEM_LIMIT),
        name="hgrn2_recurrence",
    )(proj, proj, proj, proj, lower_bounds, gnorm_w, tri)


def _mixout_kernel(seq, x_ref, oa_ref, b_ref, c_ref, u_ref, ga_ref, gb_ref,
                   bh_ref, uh_ref, cw_ref, wph_ref, wpc_ref, wo_ref, o_ref):
    i = pl.program_id(0)
    v = b_ref[...].astype(F32) * u_ref[...].astype(F32)
    at_seq_start = (i * TM_MIX) % seq == 0
    halo = bh_ref[...].astype(F32) * uh_ref[...].astype(F32)
    halo = jnp.where(at_seq_start, 0.0, halo)
    prev1 = halo[BF16_SUBLANES - 1:BF16_SUBLANES, :]
    prev2 = halo[BF16_SUBLANES - 2:BF16_SUBLANES - 1, :]
    row = lax.broadcasted_iota(I32, v.shape, 0)
    v1 = jnp.where(row == 0, prev1, pltpu.roll(v, 1, 0))
    v2 = jnp.where(row == 0, prev2,
                   jnp.where(row == 1, prev1, pltpu.roll(v, 2, 0)))
    cw = cw_ref[...]
    conv = cw[0:1, :] * v2 + cw[1:2, :] * v1 + cw[2:3, :] * v
    yb_in = (c_ref[...].astype(F32) * conv).astype(BF16)
    y_b = jnp.dot(yb_in, wpc_ref[...], preferred_element_type=F32)
    y_a = jnp.dot(oa_ref[...], wph_ref[...], preferred_element_type=F32)
    merged = (_sigmoid(ga_ref[...].astype(F32)) * y_a
              + _sigmoid(gb_ref[...].astype(F32)) * y_b).astype(BF16)
    o_ref[...] = x_ref[...] + jnp.dot(merged, wo_ref[...], preferred_element_type=F32)


def _mixout(x, o_gated, proj, conv_w, wph, wpc, wo, seq):
    t = x.shape[0]
    halo_per_tile = TM_MIX // BF16_SUBLANES

    def group_spec(g):
        return pl.BlockSpec((TM_MIX, D_MODEL), lambda i: (i, g))

    def halo_spec(g):
        return pl.BlockSpec(
            (BF16_SUBLANES, D_MODEL),
            lambda i: (jnp.maximum(i * halo_per_tile - 1, 0), g))

    def weight_spec():
        return pl.BlockSpec((D_MODEL, D_MODEL), lambda i: (0, 0))

    return pl.pallas_call(
        functools.partial(_mixout_kernel, seq),
        out_shape=jax.ShapeDtypeStruct((t, D_MODEL), F32),
        grid=(t // TM_MIX,),
        in_specs=[
            pl.BlockSpec((TM_MIX, D_MODEL), lambda i: (i, 0)),
            pl.BlockSpec((TM_MIX, D_MODEL), lambda i: (i, 0)),
            group_spec(G_B), group_spec(G_C), group_spec(G_U),
            group_spec(G_GA), group_spec(G_GB),
            halo_spec(G_B), halo_spec(G_U),
            pl.BlockSpec((CONV_K, D_MODEL), lambda i: (0, 0)),
            weight_spec(), weight_spec(), weight_spec(),
        ],
        out_specs=pl.BlockSpec((TM_MIX, D_MODEL), lambda i: (i, 0)),
        compiler_params=pltpu.CompilerParams(
            dimension_semantics=("arbitrary",),
            vmem_limit_bytes=VMEM_LIMIT),
        name="mixer_out",
    )(x, o_gated, proj, proj, proj, proj, proj, proj, proj, conv_w, wph, wpc, wo)


def _dense_ffn_kernel(x_ref, nw_ref, w1_ref, w3_ref, w2_ref, o_ref):
    x = x_ref[...]
    h = _rms_norm_rows(x, nw_ref[...]).astype(BF16)
    a = jnp.dot(h, w1_ref[...], preferred_element_type=F32)
    b = jnp.dot(h, w3_ref[...], preferred_element_type=F32)
    act = (_silu(a) * b).astype(BF16)
    o_ref[...] = x + jnp.dot(act, w2_ref[...], preferred_element_type=F32)


def _dense_ffn(x, norm_w, w1, w3, w2):
    t = x.shape[0]
    return pl.pallas_call(
        _dense_ffn_kernel,
        out_shape=jax.ShapeDtypeStruct((t, D_MODEL), F32),
        grid=(t // TM_FFN,),
        in_specs=[
            pl.BlockSpec((TM_FFN, D_MODEL), lambda i: (i, 0)),
            pl.BlockSpec((1, D_MODEL), lambda i: (0, 0)),
            pl.BlockSpec((D_MODEL, D_FF), lambda i: (0, 0)),
            pl.BlockSpec((D_MODEL, D_FF), lambda i: (0, 0)),
            pl.BlockSpec((D_FF, D_MODEL), lambda i: (0, 0)),
        ],
        out_specs=pl.BlockSpec((TM_FFN, D_MODEL), lambda i: (i, 0)),
        compiler_params=pltpu.CompilerParams(
            dimension_semantics=("arbitrary",),
            vmem_limit_bytes=VMEM_LIMIT),
        name="dense_ffn",
    )(x, norm_w, w1, w3, w2)


def _route_kernel(x_ref, nw_ref, rwt_ref, rwp_ref, triu_ref,
                  h_ref, meta_ref, wts_ref, cnt_ref, carry_scr):
    @pl.when(pl.program_id(0) == 0)
    def _():
        carry_scr[...] = jnp.zeros_like(carry_scr)

    h = _rms_norm_rows(x_ref[...], nw_ref[...])
    h_ref[...] = h
    h_hi, h_lo = _split_bf16(h)

    rwt_hi, rwt_lo = _split_bf16(rwt_ref[...])
    lt = _dot3(rwt_hi, rwt_lo, h_hi, h_lo, (((1,), (1,)), ((), ())))
    sub = lax.broadcasted_iota(I32, lt.shape, 0)
    m1 = jnp.max(lt, axis=0, keepdims=True)
    i1 = jnp.min(jnp.where(lt == m1, sub, N_EXPERTS), axis=0, keepdims=True)
    rest = jnp.where(sub == i1, -jnp.inf, lt)
    m2 = jnp.max(rest, axis=0, keepdims=True)
    i2 = jnp.min(jnp.where(rest == m2, sub, N_EXPERTS), axis=0, keepdims=True)
    oh1 = sub == i1
    oh2 = sub == i2
    member = jnp.where(jnp.logical_or(oh1, oh2), 1.0, 0.0)
    prefix = jnp.dot(member.astype(BF16), triu_ref[...], preferred_element_type=F32)
    carry = carry_scr[...]
    base = carry[:, 0:1] + prefix
    rank1 = jnp.sum(jnp.where(oh1, base, 0.0), axis=0, keepdims=True)
    rank2 = jnp.sum(jnp.where(oh2, base, 0.0), axis=0, keepdims=True)
    carry_new = carry + jnp.sum(member, axis=1, keepdims=True)
    carry_scr[...] = carry_new
    cnt_ref[...] = carry_new.astype(I32)
    meta_ref[...] = jnp.where(
        sub == 0, i1, jnp.where(
            sub == 1, i2, jnp.where(
                sub == 2, rank1.astype(I32), jnp.where(
                    sub == 3, rank2.astype(I32), 0))))

    rwp_hi, rwp_lo = _split_bf16(rwp_ref[...])
    lr = _dot3(h_hi, h_lo, rwp_hi, rwp_lo, (((1,), (0,)), ((), ())))
    lane = lax.broadcasted_iota(I32, lr.shape, 1)
    lr = jnp.where(lane < N_EXPERTS, lr, -jnp.inf)
    m1r = jnp.max(lr, axis=-1, keepdims=True)
    i1r = jnp.min(jnp.where(lr == m1r, lane, LANES), axis=-1, keepdims=True)
    m2r = jnp.max(jnp.where(lane == i1r, -jnp.inf, lr), axis=-1, keepdims=True)
    e2 = jnp.exp(m2r - m1r)
    w1 = 1.0 / (1.0 + e2)
    w2 = e2 * w1
    wts_ref[...] = jnp.where(lane == 0, w1, jnp.where(lane == 1, w2, 0.0))


def _route(x, norm_w, router_w):
    t = x.shape[0]
    rwt = router_w.T
    rwp = jnp.pad(router_w, ((0, 0), (0, LANES - N_EXPERTS)))
    triu = jnp.triu(jnp.ones((TM_ROUTE, TM_ROUTE), BF16), k=1)
    return pl.pallas_call(
        _route_kernel,
        out_shape=(
            jax.ShapeDtypeStruct((t, D_MODEL), F32),
            jax.ShapeDtypeStruct((SUBLANES, t), I32),
            jax.ShapeDtypeStruct((t, LANES), F32),
            jax.ShapeDtypeStruct((N_EXPERTS, LANES), I32),
        ),
        grid=(t // TM_ROUTE,),
        in_specs=[
            pl.BlockSpec((TM_ROUTE, D_MODEL), lambda i: (i, 0)),
            pl.BlockSpec((1, D_MODEL), lambda i: (0, 0)),
            pl.BlockSpec((N_EXPERTS, D_MODEL), lambda i: (0, 0)),
            pl.BlockSpec((D_MODEL, LANES), lambda i: (0, 0)),
            pl.BlockSpec((TM_ROUTE, TM_ROUTE), lambda i: (0, 0)),
        ],
        out_specs=(
            pl.BlockSpec((TM_ROUTE, D_MODEL), lambda i: (i, 0)),
            pl.BlockSpec((SUBLANES, TM_ROUTE), lambda i: (0, i)),
            pl.BlockSpec((TM_ROUTE, LANES), lambda i: (i, 0)),
            pl.BlockSpec((N_EXPERTS, LANES), lambda i: (0, 0)),
        ),
        scratch_shapes=[pltpu.VMEM((N_EXPERTS, LANES), F32)],
        compiler_params=pltpu.CompilerParams(
            dimension_semantics=("arbitrary",),
            vmem_limit_bytes=VMEM_LIMIT),
        name="moe_route",
    )(x, norm_w, rwt, rwp, triu)


def _store_rows_as_tiles(ref, value):
    n = value.shape[0]
    for c in range(ROW_TILE):
        ref[pl.ds(c, n, stride=ROW_TILE), :] = value[:, c * LANES:(c + 1) * LANES]


def _load_tile_chunk(ref, c, n):
    return ref[pl.ds(c, n, stride=ROW_TILE), :]


def _pos_copy(pos_hbm, pos_smem, sem, step, slot):
    n = pos_hbm.shape[1]
    dst = pos_smem.at[pl.ds(pl.multiple_of(slot * n, n), n)]
    return pltpu.make_async_copy(pos_hbm.at[step], dst, sem.at[slot])


def _dispatch_kernel(fill_ref, pos_hbm, h_ref, xs_hbm, pos_smem, hbuf, zbuf, psem, dsem, zsem):
    i = pl.program_id(0)
    n = pl.num_programs(0)
    slot = i % 2

    @pl.when(i == 0)
    def _():
        _pos_copy(pos_hbm, pos_smem, psem, 0, 0).start()

    _pos_copy(pos_hbm, pos_smem, psem, i, slot).wait()

    @pl.when(i + 1 < n)
    def _():
        _pos_copy(pos_hbm, pos_smem, psem, i + 1, 1 - slot).start()

    def wait_rows_from(s):
        for _ in range(TOP_K):
            pltpu.make_async_copy(hbuf.at[s], xs_hbm.at[pl.ds(0, TM_DISP * ROW_TILE)],
                                  dsem.at[s]).wait()

    @pl.when(i >= 2)
    def _():
        wait_rows_from(slot)

    _store_rows_as_tiles(hbuf.at[slot], h_ref[...])

    pos_base = slot * (TOP_K * TM_DISP)

    def issue(t, carry):
        src = hbuf.at[slot, pl.ds(pl.multiple_of(t * ROW_TILE, ROW_TILE), ROW_TILE)]
        for j in range(TOP_K):
            p = pl.multiple_of(pos_smem[pos_base + j * TM_DISP + t], ROW_TILE)
            pltpu.make_async_copy(src, xs_hbm.at[pl.ds(p, ROW_TILE)], dsem.at[slot]).start()
        return carry

    lax.fori_loop(0, TM_DISP, issue, 0, unroll=DMA_ISSUE_UNROLL)

    @pl.when(i == n - 1)
    def _():
        wait_rows_from(slot)

        @pl.when(n > 1)
        def _():
            wait_rows_from(1 - slot)

        zbuf[...] = jnp.zeros_like(zbuf)

        def zero_row(r, carry):
            r0 = pl.multiple_of(r * ROW_TILE, ROW_TILE)
            cp = pltpu.make_async_copy(zbuf.at[pl.ds(0, ROW_TILE)],
                                       xs_hbm.at[pl.ds(r0, ROW_TILE)], zsem)
            cp.start()
            cp.wait()
            return carry

        def zero_tile(k, carry):
            r0 = pl.multiple_of(k * (TM_GRP * ROW_TILE), TM_GRP * ROW_TILE)
            cp = pltpu.make_async_copy(zbuf, xs_hbm.at[pl.ds(r0, TM_GRP * ROW_TILE)], zsem)
            cp.start()
            cp.wait()
            return carry

        for e in range(N_EXPERTS):
            lax.fori_loop(fill_ref[e], fill_ref[N_EXPERTS + e], zero_row, 0)
        lax.fori_loop(fill_ref[2 * N_EXPERTS], xs_hbm.shape[0] // (TM_GRP * ROW_TILE),
                      zero_tile, 0)


def _dispatch(h, pos_tiles, fill_bounds, n_sorted_rows):
    t = h.shape[0]
    return pl.pallas_call(
        _dispatch_kernel,
        out_shape=jax.ShapeDtypeStruct((n_sorted_rows * ROW_TILE, LANES), F32),
        grid_spec=pltpu.PrefetchScalarGridSpec(
            num_scalar_prefetch=1,
            grid=(t // TM_DISP,),
            in_specs=[pl.BlockSpec(memory_space=pl.ANY),
                      pl.BlockSpec((TM_DISP, D_MODEL), lambda i, fill: (i, 0))],
            out_specs=pl.BlockSpec(memory_space=pl.ANY),
            scratch_shapes=[
                pltpu.SMEM((2 * TOP_K * TM_DISP,), I32),
                pltpu.VMEM((2, TM_DISP * ROW_TILE, LANES), F32),
                pltpu.VMEM((TM_GRP * ROW_TILE, LANES), F32),
                pltpu.SemaphoreType.DMA((2,)),
                pltpu.SemaphoreType.DMA((2,)),
                pltpu.SemaphoreType.DMA(()),
            ]),
        compiler_params=pltpu.CompilerParams(
            dimension_semantics=("arbitrary",),
            vmem_limit_bytes=VMEM_LIMIT),
        name="moe_dispatch",
    )(fill_bounds, pos_tiles, h)


def _experts_kernel(te_ref, nv_ref, xs_ref, w1_ref, w3_ref, w2_ref, y_ref, xb_scr):
    k = pl.program_id(0)

    @pl.when(k < nv_ref[0])
    def _():
        for c in range(ROW_TILE):
            xb_scr[:, c * LANES:(c + 1) * LANES] = _load_tile_chunk(
                xs_ref, c, TM_GRP).astype(BF16)
        xb = xb_scr[...]
        y = None
        for f in range(D_FF // FF_CHUNK):
            cols = slice(f * FF_CHUNK, (f + 1) * FF_CHUNK)
            a = jnp.dot(xb, w1_ref[0, :, cols], preferred_element_type=F32)
            b = jnp.dot(xb, w3_ref[0, :, cols], preferred_element_type=F32)
            act = (_silu(a) * b).astype(BF16)
            part = jnp.dot(act, w2_ref[0, cols, :], preferred_element_type=F32)
            y = part if y is None else y + part
        _store_rows_as_tiles(y_ref, y)

    @pl.when(k >= nv_ref[0])
    def _():
        y_ref[...] = jnp.zeros_like(y_ref)


def _experts(xs, tile_expert, n_valid, w1, w3, w2):
    n_rows = xs.shape[0] // ROW_TILE

    def row_map(k, te, nv):
        return (jnp.minimum(k, nv[0] - 1), 0)

    def out_map(k, te, nv):
        return (k, 0)

    def w_map(k, te, nv):
        return (te[jnp.minimum(k, nv[0] - 1)], 0, 0)

    def weight_spec(shape):
        return pl.BlockSpec((1,) + shape, w_map, pipeline_mode=pl.Buffered(1))

    return pl.pallas_call(
        _experts_kernel,
        out_shape=jax.ShapeDtypeStruct((n_rows * ROW_TILE, LANES), F32),
        grid_spec=pltpu.PrefetchScalarGridSpec(
            num_scalar_prefetch=2,
            grid=(n_rows // TM_GRP,),
            in_specs=[
                pl.BlockSpec((TM_GRP * ROW_TILE, LANES), row_map),
                weight_spec((D_MODEL, D_FF)),
                weight_spec((D_MODEL, D_FF)),
                weight_spec((D_FF, D_MODEL)),
            ],
            out_specs=pl.BlockSpec((TM_GRP * ROW_TILE, LANES), out_map),
            scratch_shapes=[pltpu.VMEM((TM_GRP, D_MODEL), BF16)]),
        compiler_params=pltpu.CompilerParams(
            dimension_semantics=("arbitrary",),
            vmem_limit_bytes=VMEM_LIMIT),
        name="moe_experts",
    )(tile_expert, n_valid, xs, w1, w3, w2)


def _combine_kernel(fuse_norm, pos_hbm, y_hbm, x_ref, wts_ref, fnw_ref, o_ref,
                    pos_smem, ybuf, psem, gsem):
    i = pl.program_id(0)
    n = pl.num_programs(0)
    slot = i % 2

    def issue_gathers(s):
        pos_base = s * (TOP_K * TM_COMB)

        def body(t, carry):
            dst_rows = pl.ds(pl.multiple_of(t * ROW_TILE, ROW_TILE), ROW_TILE)
            for j in range(TOP_K):
                p = pl.multiple_of(pos_smem[pos_base + j * TM_COMB + t], ROW_TILE)
                pltpu.make_async_copy(y_hbm.at[pl.ds(p, ROW_TILE)],
                                      ybuf.at[s, j, dst_rows], gsem.at[s]).start()
            return carry

        lax.fori_loop(0, TM_COMB, body, 0, unroll=DMA_ISSUE_UNROLL)

    @pl.when(i == 0)
    def _():
        first = _pos_copy(pos_hbm, pos_smem, psem, 0, 0)
        first.start()
        first.wait()
        issue_gathers(0)

        @pl.when(n > 1)
        def _():
            _pos_copy(pos_hbm, pos_smem, psem, 1, 1).start()

    @pl.when(i + 1 < n)
    def _():
        _pos_copy(pos_hbm, pos_smem, psem, i + 1, 1 - slot).wait()
        issue_gathers(1 - slot)

        @pl.when(i + 2 < n)
        def _():
            _pos_copy(pos_hbm, pos_smem, psem, i + 2, slot).start()

    for j in range(TOP_K):
        pltpu.make_async_copy(y_hbm.at[pl.ds(0, TM_COMB * ROW_TILE)], ybuf.at[slot, j],
                              gsem.at[slot]).wait()

    w = wts_ref[...]
    w1 = w[:, 0:1]
    w2 = w[:, 1:2]
    chunks = []
    for c in range(ROW_TILE):
        chunks.append(x_ref[:, c * LANES:(c + 1) * LANES]
                      + w1 * _load_tile_chunk(ybuf.at[slot, 0], c, TM_COMB)
                      + w2 * _load_tile_chunk(ybuf.at[slot, 1], c, TM_COMB))
    if fuse_norm:
        ss = chunks[0] * chunks[0]
        for ch in chunks[1:]:
            ss = ss + ch * ch
        inv = lax.rsqrt(jnp.sum(ss, axis=-1, keepdims=True) * (1.0 / D_MODEL) + EPS)
        fnw = fnw_ref[...]
        chunks = [ch * inv * fnw[:, c * LANES:(c + 1) * LANES] for c, ch in enumerate(chunks)]
    for c, ch in enumerate(chunks):
        o_ref[:, c * LANES:(c + 1) * LANES] = ch


def _combine(x, y, pos_tiles, wts, final_norm_w, fuse_norm):
    t = x.shape[0]
    return pl.pallas_call(
        functools.partial(_combine_kernel, fuse_norm),
        out_shape=jax.ShapeDtypeStruct((t, D_MODEL), F32),
        grid=(t // TM_COMB,),
        in_specs=[
            pl.BlockSpec(memory_space=pl.ANY),
            pl.BlockSpec(memory_space=pl.ANY),
            pl.BlockSpec((TM_COMB, D_MODEL), lambda i: (i, 0)),
            pl.BlockSpec((TM_COMB, LANES), lambda i: (i, 0)),
            pl.BlockSpec((1, D_MODEL), lambda i: (0, 0)),
        ],
        out_specs=pl.BlockSpec((TM_COMB, D_MODEL), lambda i: (i, 0)),
        scratch_shapes=[
            pltpu.SMEM((2 * TOP_K * TM_COMB,), I32),
            pltpu.VMEM((2, TOP_K, TM_COMB * ROW_TILE, LANES), F32),
            pltpu.SemaphoreType.DMA((2,)),
            pltpu.SemaphoreType.DMA((2,)),
        ],
        compiler_params=pltpu.CompilerParams(dimension_semantics=("arbitrary",)),
        name="moe_combine",
    )(pos_tiles, y, x, wts, final_norm_w)


def _tile_major(pos, tile):
    t = pos.shape[1]
    return pos.reshape(TOP_K, t // tile, tile).transpose(1, 0, 2).reshape(t // tile, TOP_K * tile)


def _moe_ffn(x, norm_w, router_w, w1, w3, w2, final_norm_w, fuse_norm):
    t = x.shape[0]
    h, meta, wts, cnt = _route(x, norm_w, router_w)

    counts = cnt[:, 0]
    padded = ((counts + TM_GRP - 1) // TM_GRP) * TM_GRP
    ends = jnp.cumsum(padded)
    offs = ends - padded
    n_sorted_rows = TOP_K * t + N_EXPERTS * TM_GRP
    tile_start = jnp.arange(n_sorted_rows // TM_GRP, dtype=I32) * TM_GRP
    tile_expert = jnp.minimum(
        jnp.sum((tile_start[:, None] >= ends[None, :]).astype(I32), axis=1), N_EXPERTS - 1)
    n_valid = (ends[-1] // TM_GRP).astype(I32).reshape(1)
    pos = (jnp.stack([offs[meta[0]] + meta[2], offs[meta[1]] + meta[3]]) * ROW_TILE).astype(I32)
    fill_bounds = jnp.concatenate([offs + counts, ends, n_valid]).astype(I32)

    xs = _dispatch(h, _tile_major(pos, TM_DISP), fill_bounds, n_sorted_rows)
    y = _experts(xs, tile_expert.astype(I32), n_valid, w1, w3, w2)
    return _combine(x, y, _tile_major(pos, TM_COMB), wts, final_norm_w, fuse_norm)


def _final_norm_kernel(x_ref, nw_ref, o_ref):
    o_ref[...] = _rms_norm_rows(x_ref[...], nw_ref[...])


def _final_norm(x, norm_w):
    t = x.shape[0]
    return pl.pallas_call(
        _final_norm_kernel,
        out_shape=jax.ShapeDtypeStruct((t, D_MODEL), F32),
        grid=(t // TM_PROJ,),
        in_specs=[
            pl.BlockSpec((TM_PROJ, D_MODEL), lambda i: (i, 0)),
            pl.BlockSpec((1, D_MODEL), lambda i: (0, 0)),
        ],
        out_specs=pl.BlockSpec((TM_PROJ, D_MODEL), lambda i: (i, 0)),
        compiler_params=pltpu.CompilerParams(dimension_semantics=("arbitrary",)),
        name="final_norm",
    )(x, norm_w)


def kernel(x, w_in, lower_bounds, hgrn_norm_w, conv_w, w_proj_hgrn, w_proj_conv, w_out,
           norm_mix, norm_ffn, dense_w1, dense_w3, dense_w2, router_w,
           expert_w1, expert_w3, expert_w2, final_norm):
    batch, seq, d = x.shape
    depth = w_in.shape[0]
    assert d == D_MODEL and seq % CT_REC == 0 and (batch * seq) % TM_PROJ == 0
    assert w_in.shape[2] == N_GROUPS * D_MODEL

    xt = x.reshape(batch * seq, d)
    tri = jnp.tril(jnp.ones((CHUNK, CHUNK), F32))
    lower_bounds = lower_bounds.astype(F32)
    final_w = final_norm.reshape(1, d)
    normed = False

    for layer in range(depth):
        proj = _inproj(xt, norm_mix[layer].reshape(1, d), w_in[layer].astype(BF16))
        o_gated = _recurrence(proj, lower_bounds, hgrn_norm_w[layer].reshape(1, HEAD_DIM),
                              tri, layer, batch, seq)
        xt = _mixout(xt, o_gated, proj, conv_w[layer],
                     w_proj_hgrn[layer].astype(BF16), w_proj_conv[layer].astype(BF16),
                     w_out[layer].astype(BF16), seq)
        j = layer // 2
        nw = norm_ffn[layer].reshape(1, d)
        if layer % 2 == 0:
            xt = _dense_ffn(xt, nw, dense_w1[j].astype(BF16), dense_w3[j].astype(BF16),
                            dense_w2[j].astype(BF16))
        else:
            normed = layer == depth - 1
            xt = _moe_ffn(xt, nw, router_w[j].astype(F32), expert_w1[j].astype(BF16),
                          expert_w3[j].astype(BF16), expert_w2[j].astype(BF16),
                          final_w, normed)
    out = xt if normed else _final_norm(xt, final_w)
    return out.reshape(batch, seq, d)
```

```python
import functools

import jax
import jax.numpy as jnp
from jax import lax
from jax.experimental import pallas as pl
from jax.experimental.pallas import tpu as pltpu

D_MODEL = 1024
HEADS = 8
HEAD_DIM = 128
HGRN_SCALE = HEAD_DIM ** -0.5
F_MIN = 1e-6
CONV_K = 3
D_FF = 2816
N_EXPERTS = 8
TOP_K = 2
EPS = 1e-6
LOG2_E = 1.4426950408889634
N_GROUPS = 9
G_Q, G_F, G_I, G_G, G_B, G_C, G_U, G_GA, G_GB = range(N_GROUPS)

LANES = 128
SUBLANES = 8
ROW_TILE = SUBLANES
BF16_SUBLANES = 16
VMEM_LIMIT = 56 * 1024 * 1024

TM_PROJ = 1024
TN_PROJ = 2304
TM_MIX = 512
TM_FFN = 512
FF_CHUNK = D_FF // 2
TM_ROUTE = 512
TM_DISP = 512
TM_GRP = 512
TM_COMB = 512
DMA_ISSUE_UNROLL = 8
CT_REC = 1024
CHUNK = 128
SUB = 32
N_SUB = CHUNK // SUB
REC_UNROLL = 2

F32 = jnp.float32
BF16 = jnp.bfloat16
I32 = jnp.int32


def _rms_norm_rows(x, w):
    ms = jnp.mean(x * x, axis=-1, keepdims=True)
    return x * lax.rsqrt(ms + EPS) * w


def _exp_neg(x):
    return jnp.exp2(x * (-LOG2_E))


def _sigmoid(x):
    return 1.0 / (1.0 + _exp_neg(x))


def _silu(x):
    return x * _sigmoid(x)


def _split_bf16(v):
    hi = v.astype(BF16)
    lo = (v - hi.astype(F32)).astype(BF16)
    return hi, lo


def _dot3(a_hi, a_lo, b_hi, b_lo, dims):
    dg = functools.partial(lax.dot_general, dimension_numbers=dims,
                           preferred_element_type=F32)
    return dg(a_hi, b_hi) + dg(a_hi, b_lo) + dg(a_lo, b_hi)


def _inproj_kernel(x_ref, nw_ref, w_ref, o_ref, h_scr):
    @pl.when(pl.program_id(1) == 0)
    def _():
        h_scr[...] = _rms_norm_rows(x_ref[...], nw_ref[...]).astype(BF16)

    o_ref[...] = jnp.dot(h_scr[...], w_ref[...],
                         preferred_element_type=F32).astype(o_ref.dtype)


def _inproj(x, norm_w, w_in_bf16):
    t = x.shape[0]
    n = w_in_bf16.shape[1]
    return pl.pallas_call(
        _inproj_kernel,
        out_shape=jax.ShapeDtypeStruct((t, n), BF16),
        grid=(t // TM_PROJ, n // TN_PROJ),
        in_specs=[
            pl.BlockSpec((TM_PROJ, D_MODEL), lambda i, j: (i, 0)),
            pl.BlockSpec((1, D_MODEL), lambda i, j: (0, 0)),
            pl.BlockSpec((D_MODEL, TN_PROJ), lambda i, j: (0, j)),
        ],
        out_specs=pl.BlockSpec((TM_PROJ, TN_PROJ), lambda i, j: (i, j)),
        scratch_shapes=[pltpu.VMEM((TM_PROJ, D_MODEL), BF16)],
        compiler_params=pltpu.CompilerParams(
            dimension_semantics=("arbitrary", "arbitrary"),
            vmem_limit_bytes=VMEM_LIMIT),
        name="inproj",
    )(x, norm_w, w_in_bf16)


def _layer_lower_bound(lb_raw, layer):
    m = jnp.max(lb_raw, axis=0, keepdims=True)
    e = jnp.exp(lb_raw - m)
    soft = e / jnp.sum(e, axis=0, keepdims=True)
    acc = soft[0:1, :]
    for j in range(1, layer + 1):
        acc = acc + soft[j:j + 1, :]
    return acc - soft[0:1, :]


def _rec_chunk_head(q_raw, z, v_bf, g_raw, lb, gw, st, tri_f32, tri_bf):
    qh = _silu(q_raw)
    e = _exp_neg(jnp.abs(z))
    r = 1.0 / (1.0 + e)
    er = e * r
    pos = z >= 0.0
    sig_pos = jnp.where(pos, r, er)
    sig_neg = jnp.where(pos, er, r)
    one_m_lb = 1.0 - lb
    f = lb + one_m_lb * sig_pos
    logf = jnp.log2(jnp.maximum(f, F_MIN))
    kk = one_m_lb * sig_neg

    hi, lo = _split_bf16(logf)
    cum2 = jnp.dot(tri_bf, jnp.concatenate([hi, lo], axis=1),
                   preferred_element_type=F32)
    cum = cum2[:, :HEAD_DIM] + cum2[:, HEAD_DIM:]

    s_ref = [jnp.zeros((1, HEAD_DIM), F32)]
    for j in range(1, N_SUB):
        s_ref.append(cum[j * SUB - 1:j * SUB, :])
    cum_end = cum[CHUNK - 1:CHUNK, :]

    q_loc, k_loc = [], []
    for j in range(N_SUB):
        rows = slice(j * SUB, (j + 1) * SUB)
        q_loc.append(qh[rows] * jnp.exp2(cum[rows] - s_ref[j]))
        k_loc.append(kk[rows] * jnp.exp2(s_ref[j] - cum[rows]))

    zeros_blk = jnp.zeros((SUB, HEAD_DIM), BF16)
    lhs_segs, rhs_segs = [], []
    for j in range(N_SUB):
        lhs_rows, rhs_rows = [], []
        for i in range(N_SUB):
            if i < j:
                lhs_rows.append(zeros_blk)
            elif i == j:
                lhs_rows.append(q_loc[i].astype(BF16))
            else:
                lhs_rows.append((q_loc[i] * jnp.exp2(s_ref[i] - s_ref[j])).astype(BF16))
            rhs_rows.append(k_loc[i].astype(BF16) if i == j else zeros_blk)
        lhs_segs.append(jnp.concatenate(lhs_rows, axis=0))
        rhs_segs.append(jnp.concatenate(rhs_rows, axis=0))
    lhs = jnp.concatenate(lhs_segs, axis=1)
    rhs = jnp.concatenate(rhs_segs, axis=1)
    scores = lax.dot_general(lhs, rhs, (((1,), (1,)), ((), ())),
                             preferred_element_type=F32) * tri_f32
    q_inter = lhs_segs[0]
    o = (jnp.dot(scores.astype(BF16), v_bf, preferred_element_type=F32)
         + lax.dot_general(q_inter, st.astype(BF16), (((1,), (1,)), ((), ())),
                           preferred_element_type=F32))

    k_state = jnp.concatenate(
        [(k_loc[j] * jnp.exp2(cum_end - s_ref[j])).astype(BF16) for j in range(N_SUB)],
        axis=0)
    st_new = st * jnp.exp2(cum_end) + lax.dot_general(
        v_bf, k_state, (((0,), (0,)), ((), ())), preferred_element_type=F32)

    ms = jnp.mean(o * o, axis=-1, keepdims=True)
    out = o * lax.rsqrt(ms + EPS / (HGRN_SCALE * HGRN_SCALE)) * gw * _silu(g_raw)
    return out, st_new


def _rec_kernel(layer, q_ref, f_ref, i_ref, g_ref, lbraw_ref, gw_ref, tri_ref,
                o_ref, st_ref):
    @pl.when(pl.program_id(1) == 0)
    def _():
        st_ref[...] = jnp.zeros_like(st_ref)

    lb_all = _layer_lower_bound(lbraw_ref[...], layer)
    gw = gw_ref[...]
    tri_f32 = tri_ref[...]
    tri_bf = tri_f32.astype(BF16)

    def chunk_body(c, carry):
        r0 = pl.multiple_of(c * CHUNK, CHUNK)
        rows = pl.ds(r0, CHUNK)
        for h in range(HEADS):
            cols = slice(h * HEAD_DIM, (h + 1) * HEAD_DIM)
            out, st_new = _rec_chunk_head(
                q_ref[rows, cols].astype(F32), f_ref[rows, cols].astype(F32),
                i_ref[rows, cols], g_ref[rows, cols].astype(F32),
                lb_all[:, cols], gw, st_ref[h], tri_f32, tri_bf)
            st_ref[h] = st_new
            o_ref[rows, cols] = out.astype(o_ref.dtype)
        return carry

    lax.fori_loop(0, CT_REC // CHUNK, chunk_body, 0, unroll=REC_UNROLL)


def _recurrence(proj, lower_bounds, gnorm_w, tri, layer, batch, seq):
    t = proj.shape[0]
    steps = seq // CT_REC

    def group_spec(g):
        return pl.BlockSpec((CT_REC, D_MODEL), lambda b, s: (b * steps + s, g))

    return pl.pallas_call(
        functools.partial(_rec_kernel, layer),
        out_shape=jax.ShapeDtypeStruct((t, D_MODEL), BF16),
        grid=(batch, steps),
        in_specs=[
            group_spec(G_Q), group_spec(G_F), group_spec(G_I), group_spec(G_G),
            pl.BlockSpec(lower_bounds.shape, lambda b, s: (0, 0)),
            pl.BlockSpec((1, HEAD_DIM), lambda b, s: (0, 0)),
            pl.BlockSpec((CHUNK, CHUNK), lambda b, s: (0, 0)),
        ],
        out_specs=pl.BlockSpec((CT_REC, D_MODEL), lambda b, s: (b * steps + s, 0)),
        scratch_shapes=[pltpu.VMEM((HEADS, HEAD_DIM, HEAD_DIM), F32)],
        compiler_params=pltpu.CompilerParams(
            dimension_semantics=("arbitrary", "arbitrary"),
            vmem_limit_bytes=VMEM_LIMIT),
        name="hgrn2_recurrence",
    )(proj, proj, proj, proj, lower_bounds, gnorm_w, tri)


def _mixout_kernel(seq, x_ref, oa_ref, b_ref, c_ref, u_ref, ga_ref, gb_ref,
                   bh_ref, uh_ref, cw_ref, wph_ref, wpc_ref, wo_ref, o_ref):
    i = pl.program_id(0)
    v = b_ref[...].astype(F32) * u_ref[...].astype(F32)
    at_seq_start = (i * TM_MIX) % seq == 0
    halo = bh_ref[...].astype(F32) * uh_ref[...].astype(F32)
    halo = jnp.where(at_seq_start, 0.0, halo)
    prev1 = halo[BF16_SUBLANES - 1:BF16_SUBLANES, :]
    prev2 = halo[BF16_SUBLANES - 2:BF16_SUBLANES - 1, :]
    row = lax.broadcasted_iota(I32, v.shape, 0)
    v1 = jnp.where(row == 0, prev1, pltpu.roll(v, 1, 0))
    v2 = jnp.where(row == 0, prev2,
                   jnp.where(row == 1, prev1, pltpu.roll(v, 2, 0)))
    cw = cw_ref[...]
    conv = cw[0:1, :] * v2 + cw[1:2, :] * v1 + cw[2:3, :] * v
    yb_in = (c_ref[...].astype(F32) * conv).astype(BF16)
    y_b = jnp.dot(yb_in, wpc_ref[...], preferred_element_type=F32)
    y_a = jnp.dot(oa_ref[...], wph_ref[...], preferred_element_type=F32)
    merged = (_sigmoid(ga_ref[...].astype(F32)) * y_a
              + _sigmoid(gb_ref[...].astype(F32)) * y_b).astype(BF16)
    o_ref[...] = x_ref[...] + jnp.dot(merged, wo_ref[...], preferred_element_type=F32)


def _mixout(x, o_gated, proj, conv_w, wph, wpc, wo, seq):
    t = x.shape[0]
    halo_per_tile = TM_MIX // BF16_SUBLANES

    def group_spec(g):
        return pl.BlockSpec((TM_MIX, D_MODEL), lambda i: (i, g))

    def halo_spec(g):
        return pl.BlockSpec(
            (BF16_SUBLANES, D_MODEL),
            lambda i: (jnp.maximum(i * halo_per_tile - 1, 0), g))

    def weight_spec():
        return pl.BlockSpec((D_MODEL, D_MODEL), lambda i: (0, 0))

    return pl.pallas_call(
        functools.partial(_mixout_kernel, seq),
        out_shape=jax.ShapeDtypeStruct((t, D_MODEL), F32),
        grid=(t // TM_MIX,),
        in_specs=[
            pl.BlockSpec((TM_MIX, D_MODEL), lambda i: (i, 0)),
            pl.BlockSpec((TM_MIX, D_MODEL), lambda i: (i, 0)),
            group_spec(G_B), group_spec(G_C), group_spec(G_U),
            group_spec(G_GA), group_spec(G_GB),
            halo_spec(G_B), halo_spec(G_U),
            pl.BlockSpec((CONV_K, D_MODEL), lambda i: (0, 0)),
            weight_spec(), weight_spec(), weight_spec(),
        ],
        out_specs=pl.BlockSpec((TM_MIX, D_MODEL), lambda i: (i, 0)),
        compiler_params=pltpu.CompilerParams(
            dimension_semantics=("arbitrary",),
            vmem_limit_bytes=VMEM_LIMIT),
        name="mixer_out",
    )(x, o_gated, proj, proj, proj, proj, proj, proj, proj, conv_w, wph, wpc, wo)


def _dense_ffn_kernel(x_ref, nw_ref, w1_ref, w3_ref, w2_ref, o_ref):
    x = x_ref[...]
    h = _rms_norm_rows(x, nw_ref[...]).astype(BF16)
    a = jnp.dot(h, w1_ref[...], preferred_element_type=F32)
    b = jnp.dot(h, w3_ref[...], preferred_element_type=F32)
    act = (_silu(a) * b).astype(BF16)
    o_ref[...] = x + jnp.dot(act, w2_ref[...], preferred_element_type=F32)


def _dense_ffn(x, norm_w, w1, w3, w2):
    t = x.shape[0]
    return pl.pallas_call(
        _dense_ffn_kernel,
        out_shape=jax.ShapeDtypeStruct((t, D_MODEL), F32),
        grid=(t // TM_FFN,),
        in_specs=[
            pl.BlockSpec((TM_FFN, D_MODEL), lambda i: (i, 0)),
            pl.BlockSpec((1, D_MODEL), lambda i: (0, 0)),
            pl.BlockSpec((D_MODEL, D_FF), lambda i: (0, 0)),
            pl.BlockSpec((D_MODEL, D_FF), lambda i: (0, 0)),
            pl.BlockSpec((D_FF, D_MODEL), lambda i: (0, 0)),
        ],
        out_specs=pl.BlockSpec((TM_FFN, D_MODEL), lambda i: (i, 0)),
        compiler_params=pltpu.CompilerParams(
            dimension_semantics=("arbitrary",),
            vmem_limit_bytes=VMEM_LIMIT),
        name="dense_ffn",
    )(x, norm_w, w1, w3, w2)


def _route_kernel(x_ref, nw_ref, rwt_ref, rwp_ref, triu_ref,
                  h_ref, meta_ref, wts_ref, cnt_ref, carry_scr):
    @pl.when(pl.program_id(0) == 0)
    def _():
        carry_scr[...] = jnp.zeros_like(carry_scr)

    h = _rms_norm_rows(x_ref[...], nw_ref[...])
    h_ref[...] = h
    h_hi, h_lo = _split_bf16(h)

    rwt_hi, rwt_lo = _split_bf16(rwt_ref[...])
    lt = _dot3(rwt_hi, rwt_lo, h_hi, h_lo, (((1,), (1,)), ((), ())))
    sub = lax.broadcasted_iota(I32, lt.shape, 0)
    m1 = jnp.max(lt, axis=0, keepdims=True)
    i1 = jnp.min(jnp.where(lt == m1, sub, N_EXPERTS), axis=0, keepdims=True)
    rest = jnp.where(sub == i1, -jnp.inf, lt)
    m2 = jnp.max(rest, axis=0, keepdims=True)
    i2 = jnp.min(jnp.where(rest == m2, sub, N_EXPERTS), axis=0, keepdims=True)
    oh1 = sub == i1
    oh2 = sub == i2
    member = jnp.where(jnp.logical_or(oh1, oh2), 1.0, 0.0)
    prefix = jnp.dot(member.astype(BF16), triu_ref[...], preferred_element_type=F32)
    carry = carry_scr[...]
    base = carry[:, 0:1] + prefix
    rank1 = jnp.sum(jnp.where(oh1, base, 0.0), axis=0, keepdims=True)
    rank2 = jnp.sum(jnp.where(oh2, base, 0.0), axis=0, keepdims=True)
    carry_new = carry + jnp.sum(member, axis=1, keepdims=True)
    carry_scr[...] = carry_new
    cnt_ref[...] = carry_new.astype(I32)
    meta_ref[...] = jnp.where(
        sub == 0, i1, jnp.where(
            sub == 1, i2, jnp.where(
                sub == 2, rank1.astype(I32), jnp.where(
                    sub == 3, rank2.astype(I32), 0))))

    rwp_hi, rwp_lo = _split_bf16(rwp_ref[...])
    lr = _dot3(h_hi, h_lo, rwp_hi, rwp_lo, (((1,), (0,)), ((), ())))
    lane = lax.broadcasted_iota(I32, lr.shape, 1)
    lr = jnp.where(lane < N_EXPERTS, lr, -jnp.inf)
    m1r = jnp.max(lr, axis=-1, keepdims=True)
    i1r = jnp.min(jnp.where(lr == m1r, lane, LANES), axis=-1, keepdims=True)
    m2r = jnp.max(jnp.where(lane == i1r, -jnp.inf, lr), axis=-1, keepdims=True)
    e2 = jnp.exp(m2r - m1r)
    w1 = 1.0 / (1.0 + e2)
    w2 = e2 * w1
    wts_ref[...] = jnp.where(lane == 0, w1, jnp.where(lane == 1, w2, 0.0))


def _route(x, norm_w, router_w):
    t = x.shape[0]
    rwt = router_w.T
    rwp = jnp.pad(router_w, ((0, 0), (0, LANES - N_EXPERTS)))
    triu = jnp.triu(jnp.ones((TM_ROUTE, TM_ROUTE), BF16), k=1)
    return pl.pallas_call(
        _route_kernel,
        out_shape=(
            jax.ShapeDtypeStruct((t, D_MODEL), F32),
            jax.ShapeDtypeStruct((SUBLANES, t), I32),
            jax.ShapeDtypeStruct((t, LANES), F32),
            jax.ShapeDtypeStruct((N_EXPERTS, LANES), I32),
        ),
        grid=(t // TM_ROUTE,),
        in_specs=[
            pl.BlockSpec((TM_ROUTE, D_MODEL), lambda i: (i, 0)),
            pl.BlockSpec((1, D_MODEL), lambda i: (0, 0)),
            pl.BlockSpec((N_EXPERTS, D_MODEL), lambda i: (0, 0)),
            pl.BlockSpec((D_MODEL, LANES), lambda i: (0, 0)),
            pl.BlockSpec((TM_ROUTE, TM_ROUTE), lambda i: (0, 0)),
        ],
        out_specs=(
            pl.BlockSpec((TM_ROUTE, D_MODEL), lambda i: (i, 0)),
            pl.BlockSpec((SUBLANES, TM_ROUTE), lambda i: (0, i)),
            pl.BlockSpec((TM_ROUTE, LANES), lambda i: (i, 0)),
            pl.BlockSpec((N_EXPERTS, LANES), lambda i: (0, 0)),
        ),
        scratch_shapes=[pltpu.VMEM((N_EXPERTS, LANES), F32)],
        compiler_params=pltpu.CompilerParams(
            dimension_semantics=("arbitrary",),
            vmem_limit_bytes=VMEM_LIMIT),
        name="moe_route",
    )(x, norm_w, rwt, rwp, triu)


def _store_rows_as_tiles(ref, value):
    n = value.shape[0]
    for c in range(ROW_TILE):
        ref[pl.ds(c, n, stride=ROW_TILE), :] = value[:, c * LANES:(c + 1) * LANES]


def _load_tile_chunk(ref, c, n):
    return ref[pl.ds(c, n, stride=ROW_TILE), :]


def _pos_copy(pos_hbm, pos_smem, sem, step, slot):
    n = pos_hbm.shape[1]
    dst = pos_smem.at[pl.ds(pl.multiple_of(slot * n, n), n)]
    return pltpu.make_async_copy(pos_hbm.at[step], dst, sem.at[slot])


def _dispatch_kernel(zt_ref, pos_hbm, h_ref, xs_hbm, pos_smem, hbuf, zbuf, psem, dsem, zsem):
    i = pl.program_id(0)
    n = pl.num_programs(0)
    slot = i % 2

    @pl.when(i == 0)
    def _():
        _pos_copy(pos_hbm, pos_smem, psem, 0, 0).start()

    _pos_copy(pos_hbm, pos_smem, psem, i, slot).wait()

    @pl.when(i + 1 < n)
    def _():
        _pos_copy(pos_hbm, pos_smem, psem, i + 1, 1 - slot).start()

    def wait_rows_from(s):
        for _ in range(TOP_K):
            pltpu.make_async_copy(hbuf.at[s], xs_hbm.at[pl.ds(0, TM_DISP * ROW_TILE)],
                                  dsem.at[s]).wait()

    @pl.when(i == 0)
    def _():
        zbuf[...] = jnp.zeros_like(zbuf)
        n_tiles = xs_hbm.shape[0] // (TM_GRP * ROW_TILE)

        def zero_tile_copy(k):
            r0 = pl.multiple_of(k * (TM_GRP * ROW_TILE), TM_GRP * ROW_TILE)
            return pltpu.make_async_copy(zbuf, xs_hbm.at[pl.ds(r0, TM_GRP * ROW_TILE)], zsem)

        def for_each_zero_tile(fn):
            for e in range(N_EXPERTS):
                @pl.when(zt_ref[N_EXPERTS + e] > 0)
                def _():
                    fn(zero_tile_copy(zt_ref[e]))

                @pl.when(zt_ref[2 * N_EXPERTS] + e < n_tiles)
                def _():
                    fn(zero_tile_copy(zt_ref[2 * N_EXPERTS] + e))

        for_each_zero_tile(lambda cp: cp.start())
        for_each_zero_tile(lambda cp: cp.wait())

    @pl.when(i >= 2)
    def _():
        wait_rows_from(slot)

    _store_rows_as_tiles(hbuf.at[slot], h_ref[...])

    pos_base = slot * (TOP_K * TM_DISP)

    def issue(t, carry):
        src = hbuf.at[slot, pl.ds(pl.multiple_of(t * ROW_TILE, ROW_TILE), ROW_TILE)]
        for j in range(TOP_K):
            p = pl.multiple_of(pos_smem[pos_base + j * TM_DISP + t], ROW_TILE)
            pltpu.make_async_copy(src, xs_hbm.at[pl.ds(p, ROW_TILE)], dsem.at[slot]).start()
        return carry

    lax.fori_loop(0, TM_DISP, issue, 0, unroll=DMA_ISSUE_UNROLL)

    @pl.when(i == n - 1)
    def _():
        wait_rows_from(slot)

        @pl.when(n > 1)
        def _():
            wait_rows_from(1 - slot)


def _dispatch(h, pos_tiles, zero_tiles, n_sorted_rows):
    t = h.shape[0]
    return pl.pallas_call(
        _dispatch_kernel,
        out_shape=jax.ShapeDtypeStruct((n_sorted_rows * ROW_TILE, LANES), F32),
        grid_spec=pltpu.PrefetchScalarGridSpec(
            num_scalar_prefetch=1,
            grid=(t // TM_DISP,),
            in_specs=[pl.BlockSpec(memory_space=pl.ANY),
                      pl.BlockSpec((TM_DISP, D_MODEL), lambda i, zt: (i, 0))],
            out_specs=pl.BlockSpec(memory_space=pl.ANY),
            scratch_shapes=[
                pltpu.SMEM((2 * TOP_K * TM_DISP,), I32),
                pltpu.VMEM((2, TM_DISP * ROW_TILE, LANES), F32),
                pltpu.VMEM((TM_GRP * ROW_TILE, LANES), F32),
                pltpu.SemaphoreType.DMA((2,)),
                pltpu.SemaphoreType.DMA((2,)),
                pltpu.SemaphoreType.DMA(()),
            ]),
        compiler_params=pltpu.CompilerParams(
            dimension_semantics=("arbitrary",),
            vmem_limit_bytes=VMEM_LIMIT),
        name="moe_dispatch",
    )(zero_tiles, pos_tiles, h)


def _experts_kernel(te_ref, nv_ref, xs_ref, w1_ref, w3_ref, w2_ref, y_ref, xb_scr):
    k = pl.program_id(0)

    @pl.when(k < nv_ref[0])
    def _():
        for c in range(ROW_TILE):
            xb_scr[:, c * LANES:(c + 1) * LANES] = _load_tile_chunk(
                xs_ref, c, TM_GRP).astype(BF16)
        xb = xb_scr[...]
        y = None
        for f in range(D_FF // FF_CHUNK):
            cols = slice(f * FF_CHUNK, (f + 1) * FF_CHUNK)
            a = jnp.dot(xb, w1_ref[0, :, cols], preferred_element_type=F32)
            b = jnp.dot(xb, w3_ref[0, :, cols], preferred_element_type=F32)
            act = (_silu(a) * b).astype(BF16)
            part = jnp.dot(act, w2_ref[0, cols, :], preferred_element_type=F32)
            y = part if y is None else y + part
        _store_rows_as_tiles(y_ref, y)

    @pl.when(k >= nv_ref[0])
    def _():
        y_ref[...] = jnp.zeros_like(y_ref)


def _experts(xs, tile_expert, n_valid, w1, w3, w2):
    n_rows = xs.shape[0] // ROW_TILE

    def row_map(k, te, nv):
        return (jnp.minimum(k, nv[0] - 1), 0)

    def out_map(k, te, nv):
        return (k, 0)

    def w_map(k, te, nv):
        return (te[jnp.minimum(k, nv[0] - 1)], 0, 0)

    def weight_spec(shape):
        return pl.BlockSpec((1,) + shape, w_map, pipeline_mode=pl.Buffered(1))

    return pl.pallas_call(
        _experts_kernel,
        out_shape=jax.ShapeDtypeStruct((n_rows * ROW_TILE, LANES), F32),
        grid_spec=pltpu.PrefetchScalarGridSpec(
            num_scalar_prefetch=2,
            grid=(n_rows // TM_GRP,),
            in_specs=[
                pl.BlockSpec((TM_GRP * ROW_TILE, LANES), row_map),
                weight_spec((D_MODEL, D_FF)),
                weight_spec((D_MODEL, D_FF)),
                weight_spec((D_FF, D_MODEL)),
            ],
            out_specs=pl.BlockSpec((TM_GRP * ROW_TILE, LANES), out_map),
            scratch_shapes=[pltpu.VMEM((TM_GRP, D_MODEL), BF16)]),
        compiler_params=pltpu.CompilerParams(
            dimension_semantics=("arbitrary",),
            vmem_limit_bytes=VMEM_LIMIT),
        name="moe_experts",
    )(tile_expert, n_valid, xs, w1, w3, w2)


def _combine_kernel(fuse_norm, pos_hbm, y_hbm, x_ref, wts_ref, fnw_ref, o_ref,
                    pos_smem, ybuf, psem, gsem):
    i = pl.program_id(0)
    n = pl.num_programs(0)
    slot = i % 2

    def issue_gathers(s):
        pos_base = s * (TOP_K * TM_COMB)

        def body(t, carry):
            dst_rows = pl.ds(pl.multiple_of(t * ROW_TILE, ROW_TILE), ROW_TILE)
            for j in range(TOP_K):
                p = pl.multiple_of(pos_smem[pos_base + j * TM_COMB + t], ROW_TILE)
                pltpu.make_async_copy(y_hbm.at[pl.ds(p, ROW_TILE)],
                                      ybuf.at[s, j, dst_rows], gsem.at[s]).start()
            return carry

        lax.fori_loop(0, TM_COMB, body, 0, unroll=DMA_ISSUE_UNROLL)

    @pl.when(i == 0)
    def _():
        first = _pos_copy(pos_hbm, pos_smem, psem, 0, 0)
        first.start()
        first.wait()
        issue_gathers(0)

        @pl.when(n > 1)
        def _():
            _pos_copy(pos_hbm, pos_smem, psem, 1, 1).start()

    @pl.when(i + 1 < n)
    def _():
        _pos_copy(pos_hbm, pos_smem, psem, i + 1, 1 - slot).wait()
        issue_gathers(1 - slot)

        @pl.when(i + 2 < n)
        def _():
            _pos_copy(pos_hbm, pos_smem, psem, i + 2, slot).start()

    for j in range(TOP_K):
        pltpu.make_async_copy(y_hbm.at[pl.ds(0, TM_COMB * ROW_TILE)], ybuf.at[slot, j],
                              gsem.at[slot]).wait()

    w = wts_ref[...]
    w1 = w[:, 0:1]
    w2 = w[:, 1:2]
    chunks = []
    for c in range(ROW_TILE):
        chunks.append(x_ref[:, c * LANES:(c + 1) * LANES]
                      + w1 * _load_tile_chunk(ybuf.at[slot, 0], c, TM_COMB)
                      + w2 * _load_tile_chunk(ybuf.at[slot, 1], c, TM_COMB))
    if fuse_norm:
        ss = chunks[0] * chunks[0]
        for ch in chunks[1:]:
            ss = ss + ch * ch
        inv = lax.rsqrt(jnp.sum(ss, axis=-1, keepdims=True) * (1.0 / D_MODEL) + EPS)
        fnw = fnw_ref[...]
        chunks = [ch * inv * fnw[:, c * LANES:(c + 1) * LANES] for c, ch in enumerate(chunks)]
    for c, ch in enumerate(chunks):
        o_ref[:, c * LANES:(c + 1) * LANES] = ch


def _combine(x, y, pos_tiles, wts, final_norm_w, fuse_norm):
    t = x.shape[0]
    return pl.pallas_call(
        functools.partial(_combine_kernel, fuse_norm),
        out_shape=jax.ShapeDtypeStruct((t, D_MODEL), F32),
        grid=(t // TM_COMB,),
        in_specs=[
            pl.BlockSpec(memory_space=pl.ANY),
            pl.BlockSpec(memory_space=pl.ANY),
            pl.BlockSpec((TM_COMB, D_MODEL), lambda i: (i, 0)),
            pl.BlockSpec((TM_COMB, LANES), lambda i: (i, 0)),
            pl.BlockSpec((1, D_MODEL), lambda i: (0, 0)),
        ],
        out_specs=pl.BlockSpec((TM_COMB, D_MODEL), lambda i: (i, 0)),
        scratch_shapes=[
            pltpu.SMEM((2 * TOP_K * TM_COMB,), I32),
            pltpu.VMEM((2, TOP_K, TM_COMB * ROW_TILE, LANES), F32),
            pltpu.SemaphoreType.DMA((2,)),
            pltpu.SemaphoreType.DMA((2,)),
        ],
        compiler_params=pltpu.CompilerParams(dimension_semantics=("arbitrary",)),
        name="moe_combine",
    )(pos_tiles, y, x, wts, final_norm_w)


def _tile_major(pos, tile):
    t = pos.shape[1]
    return pos.reshape(TOP_K, t // tile, tile).transpose(1, 0, 2).reshape(t // tile, TOP_K * tile)


def _moe_ffn(x, norm_w, router_w, w1, w3, w2, final_norm_w, fuse_norm):
    t = x.shape[0]
    h, meta, wts, cnt = _route(x, norm_w, router_w)

    counts = cnt[:, 0]
    padded = ((counts + TM_GRP - 1) // TM_GRP) * TM_GRP
    ends = jnp.cumsum(padded)
    offs = ends - padded
    n_sorted_rows = TOP_K * t + N_EXPERTS * TM_GRP
    tile_start = jnp.arange(n_sorted_rows // TM_GRP, dtype=I32) * TM_GRP
    tile_expert = jnp.minimum(
        jnp.sum((tile_start[:, None] >= ends[None, :]).astype(I32), axis=1), N_EXPERTS - 1)
    n_valid = (ends[-1] // TM_GRP).astype(I32).reshape(1)
    pos = (jnp.stack([offs[meta[0]] + meta[2], offs[meta[1]] + meta[3]]) * ROW_TILE).astype(I32)
    zero_tiles = jnp.concatenate([ends // TM_GRP - 1, padded, n_valid]).astype(I32)

    xs = _dispatch(h, _tile_major(pos, TM_DISP), zero_tiles, n_sorted_rows)
    y = _experts(xs, tile_expert.astype(I32), n_valid, w1, w3, w2)
    return _combine(x, y, _tile_major(pos, TM_COMB), wts, final_norm_w, fuse_norm)


def _final_norm_kernel(x_ref, nw_ref, o_ref):
    o_ref[...] = _rms_norm_rows(x_ref[...], nw_ref[...])


def _final_norm(x, norm_w):
    t = x.shape[0]
    return pl.pallas_call(
        _final_norm_kernel,
        out_shape=jax.ShapeDtypeStruct((t, D_MODEL), F32),
        grid=(t // TM_PROJ,),
        in_specs=[
            pl.BlockSpec((TM_PROJ, D_MODEL), lambda i: (i, 0)),
            pl.BlockSpec((1, D_MODEL), lambda i: (0, 0)),
        ],
        out_specs=pl.BlockSpec((TM_PROJ, D_MODEL), lambda i: (i, 0)),
        compiler_params=pltpu.CompilerParams(dimension_semantics=("arbitrary",)),
        name="final_norm",
    )(x, norm_w)


def kernel(x, w_in, lower_bounds, hgrn_norm_w, conv_w, w_proj_hgrn, w_proj_conv, w_out,
           norm_mix, norm_ffn, dense_w1, dense_w3, dense_w2, router_w,
           expert_w1, expert_w3, expert_w2, final_norm):
    batch, seq, d = x.shape
    depth = w_in.shape[0]
    assert d == D_MODEL and seq % CT_REC == 0 and (batch * seq) % TM_PROJ == 0
    assert w_in.shape[2] == N_GROUPS * D_MODEL

    xt = x.reshape(batch * seq, d)
    tri = jnp.tril(jnp.ones((CHUNK, CHUNK), F32))
    lower_bounds = lower_bounds.astype(F32)
    final_w = final_norm.reshape(1, d)
    normed = False

    for layer in range(depth):
        proj = _inproj(xt, norm_mix[layer].reshape(1, d), w_in[layer].astype(BF16))
        o_gated = _recurrence(proj, lower_bounds, hgrn_norm_w[layer].reshape(1, HEAD_DIM),
                              tri, layer, batch, seq)
        xt = _mixout(xt, o_gated, proj, conv_w[layer],
                     w_proj_hgrn[layer].astype(BF16), w_proj_conv[layer].astype(BF16),
                     w_out[layer].astype(BF16), seq)
        j = layer // 2
        nw = norm_ffn[layer].reshape(1, d)
        if layer % 2 == 0:
            xt = _dense_ffn(xt, nw, dense_w1[j].astype(BF16), dense_w3[j].astype(BF16),
                            dense_w2[j].astype(BF16))
        else:
            normed = layer == depth - 1
            xt = _moe_ffn(xt, nw, router_w[j].astype(F32), expert_w1[j].astype(BF16),
                          expert_w3[j].astype(BF16), expert_w2[j].astype(BF16),
                          final_w, normed)
    out = xt if normed else _final_norm(xt, final_w)
    return out.reshape(batch, seq, d)
```

```python
import functools

import jax
import jax.numpy as jnp
from jax import lax
from jax.experimental import pallas as pl
from jax.experimental.pallas import tpu as pltpu

D_MODEL = 1024
HEADS = 8
HEAD_DIM = 128
HGRN_SCALE = HEAD_DIM ** -0.5
F_MIN = 1e-6
CONV_K = 3
D_FF = 2816
N_EXPERTS = 8
TOP_K = 2
EPS = 1e-6
LOG2_E = 1.4426950408889634
N_GROUPS = 9
G_Q, G_F, G_I, G_G, G_B, G_C, G_U, G_GA, G_GB = range(N_GROUPS)

LANES = 128
SUBLANES = 8
ROW_TILE = SUBLANES
BF16_SUBLANES = 16
VMEM_LIMIT = 56 * 1024 * 1024

TM_PROJ = 1024
TN_PROJ = 2304
TM_MIX = 512
TM_FFN = 512
FF_CHUNK = D_FF // 2
TM_ROUTE = 512
TM_DISP = 512
TM_GRP = 512
TM_COMB = 512
DMA_ISSUE_UNROLL = 8
CT_REC = 1024
CHUNK = 128
SUB = 32
N_SUB = CHUNK // SUB
REC_UNROLL = 4

F32 = jnp.float32
BF16 = jnp.bfloat16
I32 = jnp.int32


def _rms_norm_rows(x, w):
    ms = jnp.mean(x * x, axis=-1, keepdims=True)
    return x * lax.rsqrt(ms + EPS) * w


def _exp_neg(x):
    return jnp.exp2(x * (-LOG2_E))


def _sigmoid(x):
    return 1.0 / (1.0 + _exp_neg(x))


def _silu(x):
    return x * _sigmoid(x)


def _split_bf16(v):
    hi = v.astype(BF16)
    lo = (v - hi.astype(F32)).astype(BF16)
    return hi, lo


def _dot3(a_hi, a_lo, b_hi, b_lo, dims):
    dg = functools.partial(lax.dot_general, dimension_numbers=dims,
                           preferred_element_type=F32)
    return dg(a_hi, b_hi) + dg(a_hi, b_lo) + dg(a_lo, b_hi)


def _inproj_kernel(x_ref, nw_ref, w_ref, o_ref, h_scr):
    @pl.when(pl.program_id(1) == 0)
    def _():
        h_scr[...] = _rms_norm_rows(x_ref[...], nw_ref[...]).astype(BF16)

    o_ref[...] = jnp.dot(h_scr[...], w_ref[...],
                         preferred_element_type=F32).astype(o_ref.dtype)


def _inproj(x, norm_w, w_in_bf16):
    t = x.shape[0]
    n = w_in_bf16.shape[1]
    return pl.pallas_call(
        _inproj_kernel,
        out_shape=jax.ShapeDtypeStruct((t, n), BF16),
        grid=(t // TM_PROJ, n // TN_PROJ),
        in_specs=[
            pl.BlockSpec((TM_PROJ, D_MODEL), lambda i, j: (i, 0)),
            pl.BlockSpec((1, D_MODEL), lambda i, j: (0, 0)),
            pl.BlockSpec((D_MODEL, TN_PROJ), lambda i, j: (0, j)),
        ],
        out_specs=pl.BlockSpec((TM_PROJ, TN_PROJ), lambda i, j: (i, j)),
        scratch_shapes=[pltpu.VMEM((TM_PROJ, D_MODEL), BF16)],
        compiler_params=pltpu.CompilerParams(
            dimension_semantics=("arbitrary", "arbitrary"),
            vmem_limit_bytes=VMEM_LIMIT),
        name="inproj",
    )(x, norm_w, w_in_bf16)


def _layer_lower_bound(lb_raw, layer):
    m = jnp.max(lb_raw, axis=0, keepdims=True)
    e = jnp.exp(lb_raw - m)
    soft = e / jnp.sum(e, axis=0, keepdims=True)
    acc = soft[0:1, :]
    for j in range(1, layer + 1):
        acc = acc + soft[j:j + 1, :]
    return acc - soft[0:1, :]


def _rec_chunk(rows, slot, q_ref, f_ref, i_ref, g_ref, o_ref, st_ref, lb, gw, tri_f32, tri_bf,
               lhs_s, rhs_s, kst_s, sg_s):
    z = f_ref[rows, :].astype(F32)
    e = _exp_neg(jnp.abs(z))
    r = 1.0 / (1.0 + e)
    er = e * r
    pos = z >= 0.0
    sig_pos = jnp.where(pos, r, er)
    sig_neg = jnp.where(pos, er, r)
    one_m_lb = 1.0 - lb
    logf = jnp.log2(jnp.maximum(lb + one_m_lb * sig_pos, F_MIN))
    kk = one_m_lb * sig_neg

    hi, lo = _split_bf16(logf)
    cum2 = jnp.dot(tri_bf, jnp.concatenate([hi, lo], axis=1), preferred_element_type=F32)
    cum = cum2[:, :D_MODEL] + cum2[:, D_MODEL:]

    s_ref = [jnp.zeros((1, D_MODEL), F32)]
    for j in range(1, N_SUB):
        s_ref.append(cum[j * SUB - 1:j * SUB, :])
    cum_end = cum[CHUNK - 1:CHUNK, :]
    decay_end = jnp.exp2(cum_end)

    qh = _silu(q_ref[rows, :].astype(F32))
    for j in range(N_SUB):
        blk = slice(j * SUB, (j + 1) * SUB)
        q_loc = qh[blk] * jnp.exp2(cum[blk] - s_ref[j])
        k_loc = kk[blk] * jnp.exp2(s_ref[j] - cum[blk])
        rhs_s[slot, blk, j * D_MODEL:(j + 1) * D_MODEL] = k_loc.astype(BF16)
        kst_s[slot, blk, :] = (k_loc * jnp.exp2(cum_end - s_ref[j])).astype(BF16)
        for jj in range(j + 1):
            val = q_loc if jj == j else q_loc * jnp.exp2(s_ref[j] - s_ref[jj])
            lhs_s[slot, blk, jj * D_MODEL:(jj + 1) * D_MODEL] = val.astype(BF16)
    sg_s[slot] = _silu(g_ref[rows, :].astype(F32))

    for h in range(HEADS):
        cols = slice(h * HEAD_DIM, (h + 1) * HEAD_DIM)
        seg_cols = [slice(j * D_MODEL + h * HEAD_DIM, j * D_MODEL + (h + 1) * HEAD_DIM)
                    for j in range(N_SUB)]
        lhs = jnp.concatenate([lhs_s[slot, :, sc] for sc in seg_cols], axis=1)
        rhs = jnp.concatenate([rhs_s[slot, :, sc] for sc in seg_cols], axis=1)
        scores = lax.dot_general(lhs, rhs, (((1,), (1,)), ((), ())),
                                 preferred_element_type=F32) * tri_f32
        v_bf = i_ref[rows, cols]
        st = st_ref[h]
        o = (jnp.dot(scores.astype(BF16), v_bf, preferred_element_type=F32)
             + lax.dot_general(lhs_s[slot, :, seg_cols[0]], st.astype(BF16),
                               (((1,), (1,)), ((), ())), preferred_element_type=F32))
        st_ref[h] = st * decay_end[:, cols] + lax.dot_general(
            v_bf, kst_s[slot, :, cols], (((0,), (0,)), ((), ())), preferred_element_type=F32)
        ms = jnp.mean(o * o, axis=-1, keepdims=True)
        out = o * lax.rsqrt(ms + EPS / (HGRN_SCALE * HGRN_SCALE)) * gw * sg_s[slot, :, cols]
        o_ref[rows, cols] = out.astype(o_ref.dtype)


def _rec_kernel(layer, q_ref, f_ref, i_ref, g_ref, lbraw_ref, gw_ref, tri_ref,
                 o_ref, st_ref, lhs_s, rhs_s, kst_s, sg_s):
    @pl.when(pl.program_id(1) == 0)
    def _():
        st_ref[...] = jnp.zeros_like(st_ref)
        lhs_s[...] = jnp.zeros_like(lhs_s)
        rhs_s[...] = jnp.zeros_like(rhs_s)

    lb = _layer_lower_bound(lbraw_ref[...], layer)
    gw = gw_ref[...]
    tri_f32 = tri_ref[...]
    tri_bf = tri_f32.astype(BF16)

    def body(cp, carry):
        for u in range(REC_UNROLL):
            r0 = pl.multiple_of((cp * REC_UNROLL + u) * CHUNK, CHUNK)
            _rec_chunk(pl.ds(r0, CHUNK), u, q_ref, f_ref, i_ref, g_ref, o_ref, st_ref,
                       lb, gw, tri_f32, tri_bf, lhs_s, rhs_s, kst_s, sg_s)
        return carry

    lax.fori_loop(0, CT_REC // (CHUNK * REC_UNROLL), body, 0)


def _recurrence(proj, lower_bounds, gnorm_w, tri, layer, batch, seq):
    t = proj.shape[0]
    steps = seq // CT_REC

    def group_spec(g):
        return pl.BlockSpec((CT_REC, D_MODEL), lambda b, s: (b * steps + s, g))

    return pl.pallas_call(
        functools.partial(_rec_kernel, layer),
        out_shape=jax.ShapeDtypeStruct((t, D_MODEL), BF16),
        grid=(batch, steps),
        in_specs=[
            group_spec(G_Q), group_spec(G_F), group_spec(G_I), group_spec(G_G),
            pl.BlockSpec(lower_bounds.shape, lambda b, s: (0, 0)),
            pl.BlockSpec((1, HEAD_DIM), lambda b, s: (0, 0)),
            pl.BlockSpec((CHUNK, CHUNK), lambda b, s: (0, 0)),
        ],
        out_specs=pl.BlockSpec((CT_REC, D_MODEL), lambda b, s: (b * steps + s, 0)),
        scratch_shapes=[
            pltpu.VMEM((HEADS, HEAD_DIM, HEAD_DIM), F32),
            pltpu.VMEM((REC_UNROLL, CHUNK, N_SUB * D_MODEL), BF16),
            pltpu.VMEM((REC_UNROLL, CHUNK, N_SUB * D_MODEL), BF16),
            pltpu.VMEM((REC_UNROLL, CHUNK, D_MODEL), BF16),
            pltpu.VMEM((REC_UNROLL, CHUNK, D_MODEL), F32),
        ],
        compiler_params=pltpu.CompilerParams(
            dimension_semantics=("arbitrary", "arbitrary"),
            vmem_limit_bytes=VMEM_LIMIT),
        name="hgrn2_recurrence",
    )(proj, proj, proj, proj, lower_bounds, gnorm_w, tri)


def _mixout_kernel(seq, x_ref, oa_ref, b_ref, c_ref, u_ref, ga_ref, gb_ref,
                   bh_ref, uh_ref, cw_ref, wph_ref, wpc_ref, wo_ref, o_ref):
    i = pl.program_id(0)
    v = b_ref[...].astype(F32) * u_ref[...].astype(F32)
    at_seq_start = (i * TM_MIX) % seq == 0
    halo = bh_ref[...].astype(F32) * uh_ref[...].astype(F32)
    halo = jnp.where(at_seq_start, 0.0, halo)
    prev1 = halo[BF16_SUBLANES - 1:BF16_SUBLANES, :]
    prev2 = halo[BF16_SUBLANES - 2:BF16_SUBLANES - 1, :]
    row = lax.broadcasted_iota(I32, v.shape, 0)
    v1 = jnp.where(row == 0, prev1, pltpu.roll(v, 1, 0))
    v2 = jnp.where(row == 0, prev2,
                   jnp.where(row == 1, prev1, pltpu.roll(v, 2, 0)))
    cw = cw_ref[...]
    conv = cw[0:1, :] * v2 + cw[1:2, :] * v1 + cw[2:3, :] * v
    yb_in = (c_ref[...].astype(F32) * conv).astype(BF16)
    y_b = jnp.dot(yb_in, wpc_ref[...], preferred_element_type=F32)
    y_a = jnp.dot(oa_ref[...], wph_ref[...], preferred_element_type=F32)
    merged = (_sigmoid(ga_ref[...].astype(F32)) * y_a
              + _sigmoid(gb_ref[...].astype(F32)) * y_b).astype(BF16)
    o_ref[...] = x_ref[...] + jnp.dot(merged, wo_ref[...], preferred_element_type=F32)


def _mixout(x, o_gated, proj, conv_w, wph, wpc, wo, seq):
    t = x.shape[0]
    halo_per_tile = TM_MIX // BF16_SUBLANES

    def group_spec(g):
        return pl.BlockSpec((TM_MIX, D_MODEL), lambda i: (i, g))

    def halo_spec(g):
        return pl.BlockSpec(
            (BF16_SUBLANES, D_MODEL),
            lambda i: (jnp.maximum(i * halo_per_tile - 1, 0), g))

    def weight_spec():
        return pl.BlockSpec((D_MODEL, D_MODEL), lambda i: (0, 0))

    return pl.pallas_call(
        functools.partial(_mixout_kernel, seq),
        out_shape=jax.ShapeDtypeStruct((t, D_MODEL), F32),
        grid=(t // TM_MIX,),
        in_specs=[
            pl.BlockSpec((TM_MIX, D_MODEL), lambda i: (i, 0)),
            pl.BlockSpec((TM_MIX, D_MODEL), lambda i: (i, 0)),
            group_spec(G_B), group_spec(G_C), group_spec(G_U),
            group_spec(G_GA), group_spec(G_GB),
            halo_spec(G_B), halo_spec(G_U),
            pl.BlockSpec((CONV_K, D_MODEL), lambda i: (0, 0)),
            weight_spec(), weight_spec(), weight_spec(),
        ],
        out_specs=pl.BlockSpec((TM_MIX, D_MODEL), lambda i: (i, 0)),
        compiler_params=pltpu.CompilerParams(
            dimension_semantics=("arbitrary",),
            vmem_limit_bytes=VMEM_LIMIT),
        name="mixer_out",
    )(x, o_gated, proj, proj, proj, proj, proj, proj, proj, conv_w, wph, wpc, wo)


def _dense_ffn_kernel(x_ref, nw_ref, w1_ref, w3_ref, w2_ref, o_ref):
    x = x_ref[...]
    h = _rms_norm_rows(x, nw_ref[...]).astype(BF16)
    a = jnp.dot(h, w1_ref[...], preferred_element_type=F32)
    b = jnp.dot(h, w3_ref[...], preferred_element_type=F32)
    act = (_silu(a) * b).astype(BF16)
    o_ref[...] = x + jnp.dot(act, w2_ref[...], preferred_element_type=F32)


def _dense_ffn(x, norm_w, w1, w3, w2):
    t = x.shape[0]
    return pl.pallas_call(
        _dense_ffn_kernel,
        out_shape=jax.ShapeDtypeStruct((t, D_MODEL), F32),
        grid=(t // TM_FFN,),
        in_specs=[
            pl.BlockSpec((TM_FFN, D_MODEL), lambda i: (i, 0)),
            pl.BlockSpec((1, D_MODEL), lambda i: (0, 0)),
            pl.BlockSpec((D_MODEL, D_FF), lambda i: (0, 0)),
            pl.BlockSpec((D_MODEL, D_FF), lambda i: (0, 0)),
            pl.BlockSpec((D_FF, D_MODEL), lambda i: (0, 0)),
        ],
        out_specs=pl.BlockSpec((TM_FFN, D_MODEL), lambda i: (i, 0)),
        compiler_params=pltpu.CompilerParams(
            dimension_semantics=("arbitrary",),
            vmem_limit_bytes=VMEM_LIMIT),
        name="dense_ffn",
    )(x, norm_w, w1, w3, w2)


def _route_kernel(x_ref, nw_ref, rwt_ref, rwp_ref, triu_ref,
                  h_ref, meta_ref, wts_ref, cnt_ref, carry_scr):
    @pl.when(pl.program_id(0) == 0)
    def _():
        carry_scr[...] = jnp.zeros_like(carry_scr)

    h = _rms_norm_rows(x_ref[...], nw_ref[...])
    h_ref[...] = h
    h_hi, h_lo = _split_bf16(h)

    rwt_hi, rwt_lo = _split_bf16(rwt_ref[...])
    lt = _dot3(rwt_hi, rwt_lo, h_hi, h_lo, (((1,), (1,)), ((), ())))
    sub = lax.broadcasted_iota(I32, lt.shape, 0)
    m1 = jnp.max(lt, axis=0, keepdims=True)
    i1 = jnp.min(jnp.where(lt == m1, sub, N_EXPERTS), axis=0, keepdims=True)
    rest = jnp.where(sub == i1, -jnp.inf, lt)
    m2 = jnp.max(rest, axis=0, keepdims=True)
    i2 = jnp.min(jnp.where(rest == m2, sub, N_EXPERTS), axis=0, keepdims=True)
    oh1 = sub == i1
    oh2 = sub == i2
    member = jnp.where(jnp.logical_or(oh1, oh2), 1.0, 0.0)
    prefix = jnp.dot(member.astype(BF16), triu_ref[...], preferred_element_type=F32)
    carry = carry_scr[...]
    base = carry[:, 0:1] + prefix
    rank1 = jnp.sum(jnp.where(oh1, base, 0.0), axis=0, keepdims=True)
    rank2 = jnp.sum(jnp.where(oh2, base, 0.0), axis=0, keepdims=True)
    carry_new = carry + jnp.sum(member, axis=1, keepdims=True)
    carry_scr[...] = carry_new
    cnt_ref[...] = carry_new.astype(I32)
    meta_ref[...] = jnp.where(
        sub == 0, i1, jnp.where(
            sub == 1, i2, jnp.where(
                sub == 2, rank1.astype(I32), jnp.where(
                    sub == 3, rank2.astype(I32), 0))))

    rwp_hi, rwp_lo = _split_bf16(rwp_ref[...])
    lr = _dot3(h_hi, h_lo, rwp_hi, rwp_lo, (((1,), (0,)), ((), ())))
    lane = lax.broadcasted_iota(I32, lr.shape, 1)
    lr = jnp.where(lane < N_EXPERTS, lr, -jnp.inf)
    m1r = jnp.max(lr, axis=-1, keepdims=True)
    i1r = jnp.min(jnp.where(lr == m1r, lane, LANES), axis=-1, keepdims=True)
    m2r = jnp.max(jnp.where(lane == i1r, -jnp.inf, lr), axis=-1, keepdims=True)
    e2 = jnp.exp(m2r - m1r)
    w1 = 1.0 / (1.0 + e2)
    w2 = e2 * w1
    wts_ref[...] = jnp.where(lane == 0, w1, jnp.where(lane == 1, w2, 0.0))


def _route(x, norm_w, router_w):
    t = x.shape[0]
    rwt = router_w.T
    rwp = jnp.pad(router_w, ((0, 0), (0, LANES - N_EXPERTS)))
    triu = jnp.triu(jnp.ones((TM_ROUTE, TM_ROUTE), BF16), k=1)
    return pl.pallas_call(
        _route_kernel,
        out_shape=(
            jax.ShapeDtypeStruct((t, D_MODEL), F32),
            jax.ShapeDtypeStruct((SUBLANES, t), I32),
            jax.ShapeDtypeStruct((t, LANES), F32),
            jax.ShapeDtypeStruct((N_EXPERTS, LANES), I32),
        ),
        grid=(t // TM_ROUTE,),
        in_specs=[
            pl.BlockSpec((TM_ROUTE, D_MODEL), lambda i: (i, 0)),
            pl.BlockSpec((1, D_MODEL), lambda i: (0, 0)),
            pl.BlockSpec((N_EXPERTS, D_MODEL), lambda i: (0, 0)),
            pl.BlockSpec((D_MODEL, LANES), lambda i: (0, 0)),
            pl.BlockSpec((TM_ROUTE, TM_ROUTE), lambda i: (0, 0)),
        ],
        out_specs=(
            pl.BlockSpec((TM_ROUTE, D_MODEL), lambda i: (i, 0)),
            pl.BlockSpec((SUBLANES, TM_ROUTE), lambda i: (0, i)),
            pl.BlockSpec((TM_ROUTE, LANES), lambda i: (i, 0)),
            pl.BlockSpec((N_EXPERTS, LANES), lambda i: (0, 0)),
        ),
        scratch_shapes=[pltpu.VMEM((N_EXPERTS, LANES), F32)],
        compiler_params=pltpu.CompilerParams(
            dimension_semantics=("arbitrary",),
            vmem_limit_bytes=VMEM_LIMIT),
        name="moe_route",
    )(x, norm_w, rwt, rwp, triu)


def _store_rows_as_tiles(ref, value):
    n = value.shape[0]
    for c in range(ROW_TILE):
        ref[pl.ds(c, n, stride=ROW_TILE), :] = value[:, c * LANES:(c + 1) * LANES]


def _load_tile_chunk(ref, c, n):
    return ref[pl.ds(c, n, stride=ROW_TILE), :]


def _pos_copy(pos_hbm, pos_smem, sem, step, slot):
    n = pos_hbm.shape[1]
    dst = pos_smem.at[pl.ds(pl.multiple_of(slot * n, n), n)]
    return pltpu.make_async_copy(pos_hbm.at[step], dst, sem.at[slot])


def _dispatch_kernel(zt_ref, pos_hbm, h_ref, xs_hbm, pos_smem, hbuf, zbuf, psem, dsem, zsem):
    i = pl.program_id(0)
    n = pl.num_programs(0)
    slot = i % 2

    @pl.when(i == 0)
    def _():
        _pos_copy(pos_hbm, pos_smem, psem, 0, 0).start()

    _pos_copy(pos_hbm, pos_smem, psem, i, slot).wait()

    @pl.when(i + 1 < n)
    def _():
        _pos_copy(pos_hbm, pos_smem, psem, i + 1, 1 - slot).start()

    def wait_rows_from(s):
        for _ in range(TOP_K):
            pltpu.make_async_copy(hbuf.at[s], xs_hbm.at[pl.ds(0, TM_DISP * ROW_TILE)],
                                  dsem.at[s]).wait()

    @pl.when(i == 0)
    def _():
        zbuf[...] = jnp.zeros_like(zbuf)
        n_tiles = xs_hbm.shape[0] // (TM_GRP * ROW_TILE)

        def zero_tile_copy(k):
            r0 = pl.multiple_of(k * (TM_GRP * ROW_TILE), TM_GRP * ROW_TILE)
            return pltpu.make_async_copy(zbuf, xs_hbm.at[pl.ds(r0, TM_GRP * ROW_TILE)], zsem)

        def for_each_zero_tile(fn):
            for e in range(N_EXPERTS):
                @pl.when(zt_ref[N_EXPERTS + e] > 0)
                def _():
                    fn(zero_tile_copy(zt_ref[e]))

                @pl.when(zt_ref[2 * N_EXPERTS] + e < n_tiles)
                def _():
                    fn(zero_tile_copy(zt_ref[2 * N_EXPERTS] + e))

        for_each_zero_tile(lambda cp: cp.start())
        for_each_zero_tile(lambda cp: cp.wait())

    @pl.when(i >= 2)
    def _():
        wait_rows_from(slot)

    _store_rows_as_tiles(hbuf.at[slot], h_ref[...])

    pos_base = slot * (TOP_K * TM_DISP)

    def issue(t, carry):
        src = hbuf.at[slot, pl.ds(pl.multiple_of(t * ROW_TILE, ROW_TILE), ROW_TILE)]
        for j in range(TOP_K):
            p = pl.multiple_of(pos_smem[pos_base + j * TM_DISP + t], ROW_TILE)
            pltpu.make_async_copy(src, xs_hbm.at[pl.ds(p, ROW_TILE)],
                                  dsem.at[slot]).start(priority=j)
        return carry

    lax.fori_loop(0, TM_DISP, issue, 0, unroll=DMA_ISSUE_UNROLL)

    @pl.when(i == n - 1)
    def _():
        wait_rows_from(slot)

        @pl.when(n > 1)
        def _():
            wait_rows_from(1 - slot)


def _dispatch(h, pos_tiles, zero_tiles, n_sorted_rows):
    t = h.shape[0]
    return pl.pallas_call(
        _dispatch_kernel,
        out_shape=jax.ShapeDtypeStruct((n_sorted_rows * ROW_TILE, LANES), F32),
        grid_spec=pltpu.PrefetchScalarGridSpec(
            num_scalar_prefetch=1,
            grid=(t // TM_DISP,),
            in_specs=[pl.BlockSpec(memory_space=pl.ANY),
                      pl.BlockSpec((TM_DISP, D_MODEL), lambda i, zt: (i, 0))],
            out_specs=pl.BlockSpec(memory_space=pl.ANY),
            scratch_shapes=[
                pltpu.SMEM((2 * TOP_K * TM_DISP,), I32),
                pltpu.VMEM((2, TM_DISP * ROW_TILE, LANES), F32),
                pltpu.VMEM((TM_GRP * ROW_TILE, LANES), F32),
                pltpu.SemaphoreType.DMA((2,)),
                pltpu.SemaphoreType.DMA((2,)),
                pltpu.SemaphoreType.DMA(()),
            ]),
        compiler_params=pltpu.CompilerParams(
            dimension_semantics=("arbitrary",),
            vmem_limit_bytes=VMEM_LIMIT),
        name="moe_dispatch",
    )(zero_tiles, pos_tiles, h)


def _experts_kernel(te_ref, nv_ref, xs_ref, w1_ref, w3_ref, w2_ref, y_ref, xb_scr):
    k = pl.program_id(0)

    @pl.when(k < nv_ref[0])
    def _():
        for c in range(ROW_TILE):
            xb_scr[:, c * LANES:(c + 1) * LANES] = _load_tile_chunk(
                xs_ref, c, TM_GRP).astype(BF16)
        xb = xb_scr[...]
        y = None
        for f in range(D_FF // FF_CHUNK):
            cols = slice(f * FF_CHUNK, (f + 1) * FF_CHUNK)
            a = jnp.dot(xb, w1_ref[0, :, cols], preferred_element_type=F32)
            b = jnp.dot(xb, w3_ref[0, :, cols], preferred_element_type=F32)
            act = (_silu(a) * b).astype(BF16)
            part = jnp.dot(act, w2_ref[0, cols, :], preferred_element_type=F32)
            y = part if y is None else y + part
        _store_rows_as_tiles(y_ref, y)

    @pl.when(k >= nv_ref[0])
    def _():
        y_ref[...] = jnp.zeros_like(y_ref)


def _experts(xs, tile_expert, n_valid, w1, w3, w2):
    n_rows = xs.shape[0] // ROW_TILE

    def row_map(k, te, nv):
        return (jnp.minimum(k, nv[0] - 1), 0)

    def out_map(k, te, nv):
        return (k, 0)

    def w_map(k, te, nv):
        return (te[jnp.minimum(k, nv[0] - 1)], 0, 0)

    def weight_spec(shape):
        return pl.BlockSpec((1,) + shape, w_map, pipeline_mode=pl.Buffered(1))

    return pl.pallas_call(
        _experts_kernel,
        out_shape=jax.ShapeDtypeStruct((n_rows * ROW_TILE, LANES), F32),
        grid_spec=pltpu.PrefetchScalarGridSpec(
            num_scalar_prefetch=2,
            grid=(n_rows // TM_GRP,),
            in_specs=[
                pl.BlockSpec((TM_GRP * ROW_TILE, LANES), row_map),
                weight_spec((D_MODEL, D_FF)),
                weight_spec((D_MODEL, D_FF)),
                weight_spec((D_FF, D_MODEL)),
            ],
            out_specs=pl.BlockSpec((TM_GRP * ROW_TILE, LANES), out_map),
            scratch_shapes=[pltpu.VMEM((TM_GRP, D_MODEL), BF16)]),
        compiler_params=pltpu.CompilerParams(
            dimension_semantics=("arbitrary",),
            vmem_limit_bytes=VMEM_LIMIT),
        name="moe_experts",
    )(tile_expert, n_valid, xs, w1, w3, w2)


def _combine_kernel(fuse_norm, pos_hbm, y_hbm, x_ref, wts_ref, fnw_ref, o_ref,
                    pos_smem, ybuf, psem, gsem):
    i = pl.program_id(0)
    n = pl.num_programs(0)
    slot = i % 2

    def issue_gathers(s):
        pos_base = s * (TOP_K * TM_COMB)

        def body(t, carry):
            dst_rows = pl.ds(pl.multiple_of(t * ROW_TILE, ROW_TILE), ROW_TILE)
            for j in range(TOP_K):
                p = pl.multiple_of(pos_smem[pos_base + j * TM_COMB + t], ROW_TILE)
                pltpu.make_async_copy(y_hbm.at[pl.ds(p, ROW_TILE)],
                                      ybuf.at[s, j, dst_rows], gsem.at[s]).start(priority=j)
            return carry

        lax.fori_loop(0, TM_COMB, body, 0, unroll=DMA_ISSUE_UNROLL)

    @pl.when(i == 0)
    def _():
        first = _pos_copy(pos_hbm, pos_smem, psem, 0, 0)
        first.start()
        first.wait()
        issue_gathers(0)

        @pl.when(n > 1)
        def _():
            _pos_copy(pos_hbm, pos_smem, psem, 1, 1).start()

    @pl.when(i + 1 < n)
    def _():
        _pos_copy(pos_hbm, pos_smem, psem, i + 1, 1 - slot).wait()
        issue_gathers(1 - slot)

        @pl.when(i + 2 < n)
        def _():
            _pos_copy(pos_hbm, pos_smem, psem, i + 2, slot).start()

    for j in range(TOP_K):
        pltpu.make_async_copy(y_hbm.at[pl.ds(0, TM_COMB * ROW_TILE)], ybuf.at[slot, j],
                              gsem.at[slot]).wait()

    w = wts_ref[...]
    w1 = w[:, 0:1]
    w2 = w[:, 1:2]
    chunks = []
    for c in range(ROW_TILE):
        chunks.append(x_ref[:, c * LANES:(c + 1) * LANES]
                      + w1 * _load_tile_chunk(ybuf.at[slot, 0], c, TM_COMB)
                      + w2 * _load_tile_chunk(ybuf.at[slot, 1], c, TM_COMB))
    if fuse_norm:
        ss = chunks[0] * chunks[0]
        for ch in chunks[1:]:
            ss = ss + ch * ch
        inv = lax.rsqrt(jnp.sum(ss, axis=-1, keepdims=True) * (1.0 / D_MODEL) + EPS)
        fnw = fnw_ref[...]
        chunks = [ch * inv * fnw[:, c * LANES:(c + 1) * LANES] for c, ch in enumerate(chunks)]
    for c, ch in enumerate(chunks):
        o_ref[:, c * LANES:(c + 1) * LANES] = ch


def _combine(x, y, pos_tiles, wts, final_norm_w, fuse_norm):
    t = x.shape[0]
    return pl.pallas_call(
        functools.partial(_combine_kernel, fuse_norm),
        out_shape=jax.ShapeDtypeStruct((t, D_MODEL), F32),
        grid=(t // TM_COMB,),
        in_specs=[
            pl.BlockSpec(memory_space=pl.ANY),
            pl.BlockSpec(memory_space=pl.ANY),
            pl.BlockSpec((TM_COMB, D_MODEL), lambda i: (i, 0)),
            pl.BlockSpec((TM_COMB, LANES), lambda i: (i, 0)),
            pl.BlockSpec((1, D_MODEL), lambda i: (0, 0)),
        ],
        out_specs=pl.BlockSpec((TM_COMB, D_MODEL), lambda i: (i, 0)),
        scratch_shapes=[
            pltpu.SMEM((2 * TOP_K * TM_COMB,), I32),
            pltpu.VMEM((2, TOP_K, TM_COMB * ROW_TILE, LANES), F32),
            pltpu.SemaphoreType.DMA((2,)),
            pltpu.SemaphoreType.DMA((2,)),
        ],
        compiler_params=pltpu.CompilerParams(dimension_semantics=("arbitrary",)),
        name="moe_combine",
    )(pos_tiles, y, x, wts, final_norm_w)


def _tile_major(pos, tile):
    t = pos.shape[1]
    return pos.reshape(TOP_K, t // tile, tile).transpose(1, 0, 2).reshape(t // tile, TOP_K * tile)


def _moe_ffn(x, norm_w, router_w, w1, w3, w2, final_norm_w, fuse_norm):
    t = x.shape[0]
    h, meta, wts, cnt = _route(x, norm_w, router_w)

    counts = cnt[:, 0]
    padded = ((counts + TM_GRP - 1) // TM_GRP) * TM_GRP
    ends = jnp.cumsum(padded)
    offs = ends - padded
    n_sorted_rows = TOP_K * t + N_EXPERTS * TM_GRP
    tile_start = jnp.arange(n_sorted_rows // TM_GRP, dtype=I32) * TM_GRP
    tile_expert = jnp.minimum(
        jnp.sum((tile_start[:, None] >= ends[None, :]).astype(I32), axis=1), N_EXPERTS - 1)
    n_valid = (ends[-1] // TM_GRP).astype(I32).reshape(1)
    pos = (jnp.stack([offs[meta[0]] + meta[2], offs[meta[1]] + meta[3]]) * ROW_TILE).astype(I32)
    zero_tiles = jnp.concatenate([ends // TM_GRP - 1, padded, n_valid]).astype(I32)

    xs = _dispatch(h, _tile_major(pos, TM_DISP), zero_tiles, n_sorted_rows)
    y = _experts(xs, tile_expert.astype(I32), n_valid, w1, w3, w2)
    return _combine(x, y, _tile_major(pos, TM_COMB), wts, final_norm_w, fuse_norm)


def _final_norm_kernel(x_ref, nw_ref, o_ref):
    o_ref[...] = _rms_norm_rows(x_ref[...], nw_ref[...])


def _final_norm(x, norm_w):
    t = x.shape[0]
    return pl.pallas_call(
        _final_norm_kernel,
        out_shape=jax.ShapeDtypeStruct((t, D_MODEL), F32),
        grid=(t // TM_PROJ,),
        in_specs=[
            pl.BlockSpec((TM_PROJ, D_MODEL), lambda i: (i, 0)),
            pl.BlockSpec((1, D_MODEL), lambda i: (0, 0)),
        ],
        out_specs=pl.BlockSpec((TM_PROJ, D_MODEL), lambda i: (i, 0)),
        compiler_params=pltpu.CompilerParams(dimension_semantics=("arbitrary",)),
        name="final_norm",
    )(x, norm_w)


def kernel(x, w_in, lower_bounds, hgrn_norm_w, conv_w, w_proj_hgrn, w_proj_conv, w_out,
           norm_mix, norm_ffn, dense_w1, dense_w3, dense_w2, router_w,
           expert_w1, expert_w3, expert_w2, final_norm):
    batch, seq, d = x.shape
    depth = w_in.shape[0]
    assert d == D_MODEL and seq % CT_REC == 0 and (batch * seq) % TM_PROJ == 0
    assert w_in.shape[2] == N_GROUPS * D_MODEL

    xt = x.reshape(batch * seq, d)
    tri = jnp.tril(jnp.ones((CHUNK, CHUNK), F32))
    lower_bounds = lower_bounds.astype(F32)
    final_w = final_norm.reshape(1, d)
    normed = False

    for layer in range(depth):
        proj = _inproj(xt, norm_mix[layer].reshape(1, d), w_in[layer].astype(BF16))
        o_gated = _recurrence(proj, lower_bounds, hgrn_norm_w[layer].reshape(1, HEAD_DIM),
                              tri, layer, batch, seq)
        xt = _mixout(xt, o_gated, proj, conv_w[layer],
                     w_proj_hgrn[layer].astype(BF16), w_proj_conv[layer].astype(BF16),
                     w_out[layer].astype(BF16), seq)
        j = layer // 2
        nw = norm_ffn[layer].reshape(1, d)
        if layer % 2 == 0:
            xt = _dense_ffn(xt, nw, dense_w1[j].astype(BF16), dense_w3[j].astype(BF16),
                            dense_w2[j].astype(BF16))
        else:
            normed = layer == depth - 1
            xt = _moe_ffn(xt, nw, router_w[j].astype(F32), expert_w1[j].astype(BF16),
                          expert_w3[j].astype(BF16), expert_w2[j].astype(BF16),
                          final_w, normed)
    out = xt if normed else _final_norm(xt, final_w)
    return out.reshape(batch, seq, d)
```

```python
import functools

import jax
import jax.numpy as jnp
from jax import lax
from jax.experimental import pallas as pl
from jax.experimental.pallas import tpu as pltpu

D_MODEL = 1024
HEADS = 8
HEAD_DIM = 128
HGRN_SCALE = HEAD_DIM ** -0.5
F_MIN = 1e-6
CONV_K = 3
D_FF = 2816
N_EXPERTS = 8
TOP_K = 2
EPS = 1e-6
LOG2_E = 1.4426950408889634
N_GROUPS = 9
G_Q, G_F, G_I, G_G, G_B, G_C, G_U, G_GA, G_GB = range(N_GROUPS)

LANES = 128
SUBLANES = 8
ROW_TILE = SUBLANES
BF16_SUBLANES = 16
VMEM_LIMIT = 56 * 1024 * 1024

TM_PROJ = 1024
TN_PROJ = 2304
TM_MIX = 512
TM_FFN = 512
MXU_TILE = 256
FF_SPLITS = (0, 6 * MXU_TILE, D_FF)
TM_ROUTE = 512
TM_DISP = 512
TM_GRP = 512
TM_COMB = 512
DMA_ISSUE_UNROLL = 8
CT_REC = 1024
CHUNK = 128
SUB = 32
N_SUB = CHUNK // SUB
MAX_EXP2 = 126.0
REC_UNROLL = 4

F32 = jnp.float32
BF16 = jnp.bfloat16
I32 = jnp.int32


def _rms_norm_rows(x, w):
    ms = jnp.mean(x * x, axis=-1, keepdims=True)
    return x * lax.rsqrt(ms + EPS) * w


def _exp_neg(x):
    return jnp.exp2(x * (-LOG2_E))


def _sigmoid(x):
    return 1.0 / (1.0 + _exp_neg(x))


def _silu(x):
    return x * _sigmoid(x)


def _split_bf16(v):
    hi = v.astype(BF16)
    lo = (v - hi.astype(F32)).astype(BF16)
    return hi, lo


def _dot3(a_hi, a_lo, b_hi, b_lo, dims):
    dg = functools.partial(lax.dot_general, dimension_numbers=dims,
                           preferred_element_type=F32)
    return dg(a_hi, b_hi) + dg(a_hi, b_lo) + dg(a_lo, b_hi)


def _inproj_kernel(x_ref, nw_ref, w_ref, o_ref, h_scr):
    @pl.when(pl.program_id(1) == 0)
    def _():
        h = _rms_norm_rows(x_ref[...], nw_ref[...]).astype(BF16)
        h_scr[...] = h
        o_ref[...] = jnp.dot(h, w_ref[...], preferred_element_type=F32).astype(o_ref.dtype)

    @pl.when(pl.program_id(1) > 0)
    def _():
        o_ref[...] = jnp.dot(h_scr[...], w_ref[...],
                             preferred_element_type=F32).astype(o_ref.dtype)


def _inproj(x, norm_w, w_in_bf16):
    t = x.shape[0]
    n = w_in_bf16.shape[1]
    return pl.pallas_call(
        _inproj_kernel,
        out_shape=jax.ShapeDtypeStruct((t, n), BF16),
        grid=(t // TM_PROJ, n // TN_PROJ),
        in_specs=[
            pl.BlockSpec((TM_PROJ, D_MODEL), lambda i, j: (i, 0)),
            pl.BlockSpec((1, D_MODEL), lambda i, j: (0, 0)),
            pl.BlockSpec((D_MODEL, TN_PROJ), lambda i, j: (0, j)),
        ],
        out_specs=pl.BlockSpec((TM_PROJ, TN_PROJ), lambda i, j: (i, j)),
        scratch_shapes=[pltpu.VMEM((TM_PROJ, D_MODEL), BF16)],
        compiler_params=pltpu.CompilerParams(
            dimension_semantics=("arbitrary", "arbitrary"),
            vmem_limit_bytes=VMEM_LIMIT),
        name="inproj",
    )(x, norm_w, w_in_bf16)


def _layer_lower_bound(lb_raw, layer):
    m = jnp.max(lb_raw, axis=0, keepdims=True)
    e = jnp.exp(lb_raw - m)
    soft = e / jnp.sum(e, axis=0, keepdims=True)
    acc = soft[0:1, :]
    for j in range(1, layer + 1):
        acc = acc + soft[j:j + 1, :]
    return acc - soft[0:1, :]


def _rec_chunk(rows, slot, q_ref, f_ref, i_ref, g_ref, o_ref, st_ref, lb, gw, tri_f32, tri_bf,
               lhs_s, rhs_s, kst_s, sg_s):
    z = f_ref[rows, :].astype(F32)
    e = _exp_neg(jnp.abs(z))
    r = 1.0 / (1.0 + e)
    er = e * r
    pos = z >= 0.0
    sig_pos = jnp.where(pos, r, er)
    sig_neg = jnp.where(pos, er, r)
    one_m_lb = 1.0 - lb
    logf = jnp.log2(jnp.maximum(lb + one_m_lb * sig_pos, F_MIN))
    kk = one_m_lb * sig_neg

    hi, lo = _split_bf16(logf)
    cum2 = jnp.dot(tri_bf, jnp.concatenate([hi, lo], axis=1), preferred_element_type=F32)
    cum = cum2[:, :D_MODEL] + cum2[:, D_MODEL:]

    s_ref = [jnp.zeros((1, D_MODEL), F32)]
    for j in range(1, N_SUB):
        s_ref.append(cum[j * SUB - 1:j * SUB, :])
    cum_end = cum[CHUNK - 1:CHUNK, :]
    decay_end = jnp.exp2(cum_end)

    qh = _silu(q_ref[rows, :].astype(F32))
    for j in range(N_SUB):
        blk = slice(j * SUB, (j + 1) * SUB)
        q_loc = qh[blk] * jnp.exp2(cum[blk] - s_ref[j])
        k_loc = kk[blk] * jnp.exp2(jnp.minimum(s_ref[j] - cum[blk], MAX_EXP2))
        rhs_s[slot, blk, j * D_MODEL:(j + 1) * D_MODEL] = k_loc.astype(BF16)
        kst_s[slot, blk, :] = (k_loc * jnp.exp2(cum_end - s_ref[j])).astype(BF16)
        for jj in range(j + 1):
            val = q_loc if jj == j else q_loc * jnp.exp2(s_ref[j] - s_ref[jj])
            lhs_s[slot, blk, jj * D_MODEL:(jj + 1) * D_MODEL] = val.astype(BF16)
    sg_s[slot] = _silu(g_ref[rows, :].astype(F32))

    for h in range(HEADS):
        cols = slice(h * HEAD_DIM, (h + 1) * HEAD_DIM)
        seg_cols = [slice(j * D_MODEL + h * HEAD_DIM, j * D_MODEL + (h + 1) * HEAD_DIM)
                    for j in range(N_SUB)]
        lhs = jnp.concatenate([lhs_s[slot, :, sc] for sc in seg_cols], axis=1)
        rhs = jnp.concatenate([rhs_s[slot, :, sc] for sc in seg_cols], axis=1)
        scores = lax.dot_general(lhs, rhs, (((1,), (1,)), ((), ())),
                                 preferred_element_type=F32) * tri_f32
        v_bf = i_ref[rows, cols]
        st = st_ref[h]
        o = (jnp.dot(scores.astype(BF16), v_bf, preferred_element_type=F32)
             + lax.dot_general(lhs_s[slot, :, seg_cols[0]], st.astype(BF16),
                               (((1,), (1,)), ((), ())), preferred_element_type=F32))
        st_ref[h] = st * decay_end[:, cols] + lax.dot_general(
            v_bf, kst_s[slot, :, cols], (((0,), (0,)), ((), ())), preferred_element_type=F32)
        ms = jnp.mean(o * o, axis=-1, keepdims=True)
        out = o * lax.rsqrt(ms + EPS / (HGRN_SCALE * HGRN_SCALE)) * gw * sg_s[slot, :, cols]
        o_ref[rows, cols] = out.astype(o_ref.dtype)


def _rec_kernel(layer, q_ref, f_ref, i_ref, g_ref, lbraw_ref, gw_ref, tri_ref,
                 o_ref, st_ref, lhs_s, rhs_s, kst_s, sg_s):
    @pl.when(pl.program_id(1) == 0)
    def _():
        st_ref[...] = jnp.zeros_like(st_ref)
        lhs_s[...] = jnp.zeros_like(lhs_s)
        rhs_s[...] = jnp.zeros_like(rhs_s)

    lb = _layer_lower_bound(lbraw_ref[...], layer)
    gw = gw_ref[...]
    tri_f32 = tri_ref[...]
    tri_bf = tri_f32.astype(BF16)

    def body(cp, carry):
        for u in range(REC_UNROLL):
            r0 = pl.multiple_of((cp * REC_UNROLL + u) * CHUNK, CHUNK)
            _rec_chunk(pl.ds(r0, CHUNK), u, q_ref, f_ref, i_ref, g_ref, o_ref, st_ref,
                       lb, gw, tri_f32, tri_bf, lhs_s, rhs_s, kst_s, sg_s)
        return carry

    lax.fori_loop(0, CT_REC // (CHUNK * REC_UNROLL), body, 0)


def _recurrence(proj, lower_bounds, gnorm_w, tri, layer, batch, seq):
    t = proj.shape[0]
    steps = seq // CT_REC

    def group_spec(g):
        return pl.BlockSpec((CT_REC, D_MODEL), lambda b, s: (b * steps + s, g))

    return pl.pallas_call(
        functools.partial(_rec_kernel, layer),
        out_shape=jax.ShapeDtypeStruct((t, D_MODEL), BF16),
        grid=(batch, steps),
        in_specs=[
            group_spec(G_Q), group_spec(G_F), group_spec(G_I), group_spec(G_G),
            pl.BlockSpec(lower_bounds.shape, lambda b, s: (0, 0)),
            pl.BlockSpec((1, HEAD_DIM), lambda b, s: (0, 0)),
            pl.BlockSpec((CHUNK, CHUNK), lambda b, s: (0, 0)),
        ],
        out_specs=pl.BlockSpec((CT_REC, D_MODEL), lambda b, s: (b * steps + s, 0)),
        scratch_shapes=[
            pltpu.VMEM((HEADS, HEAD_DIM, HEAD_DIM), F32),
            pltpu.VMEM((REC_UNROLL, CHUNK, N_SUB * D_MODEL), BF16),
            pltpu.VMEM((REC_UNROLL, CHUNK, N_SUB * D_MODEL), BF16),
            pltpu.VMEM((REC_UNROLL, CHUNK, D_MODEL), BF16),
            pltpu.VMEM((REC_UNROLL, CHUNK, D_MODEL), F32),
        ],
        compiler_params=pltpu.CompilerParams(
            dimension_semantics=("arbitrary", "arbitrary"),
            vmem_limit_bytes=VMEM_LIMIT),
        name="hgrn2_recurrence",
    )(proj, proj, proj, proj, lower_bounds, gnorm_w, tri)


def _mixout_kernel(seq, x_ref, oa_ref, b_ref, c_ref, u_ref, ga_ref, gb_ref,
                   bh_ref, uh_ref, cw_ref, wph_ref, wpc_ref, wo_ref, o_ref):
    i = pl.program_id(0)
    v = b_ref[...].astype(F32) * u_ref[...].astype(F32)
    at_seq_start = (i * TM_MIX) % seq == 0
    halo = bh_ref[...].astype(F32) * uh_ref[...].astype(F32)
    halo = jnp.where(at_seq_start, 0.0, halo)
    prev1 = halo[BF16_SUBLANES - 1:BF16_SUBLANES, :]
    prev2 = halo[BF16_SUBLANES - 2:BF16_SUBLANES - 1, :]
    row = lax.broadcasted_iota(I32, v.shape, 0)
    v1 = jnp.where(row == 0, prev1, pltpu.roll(v, 1, 0))
    v2 = jnp.where(row == 0, prev2,
                   jnp.where(row == 1, prev1, pltpu.roll(v, 2, 0)))
    cw = cw_ref[...]
    conv = cw[0:1, :] * v2 + cw[1:2, :] * v1 + cw[2:3, :] * v
    yb_in = (c_ref[...].astype(F32) * conv).astype(BF16)
    y_b = jnp.dot(yb_in, wpc_ref[...], preferred_element_type=F32)
    y_a = jnp.dot(oa_ref[...], wph_ref[...], preferred_element_type=F32)
    merged = (_sigmoid(ga_ref[...].astype(F32)) * y_a
              + _sigmoid(gb_ref[...].astype(F32)) * y_b).astype(BF16)
    o_ref[...] = x_ref[...] + jnp.dot(merged, wo_ref[...], preferred_element_type=F32)


def _mixout(x, o_gated, proj, conv_w, wph, wpc, wo, seq):
    t = x.shape[0]
    halo_per_tile = TM_MIX // BF16_SUBLANES

    def group_spec(g):
        return pl.BlockSpec((TM_MIX, D_MODEL), lambda i: (i, g))

    def halo_spec(g):
        return pl.BlockSpec(
            (BF16_SUBLANES, D_MODEL),
            lambda i: (jnp.maximum(i * halo_per_tile - 1, 0), g))

    def weight_spec():
        return pl.BlockSpec((D_MODEL, D_MODEL), lambda i: (0, 0))

    return pl.pallas_call(
        functools.partial(_mixout_kernel, seq),
        out_shape=jax.ShapeDtypeStruct((t, D_MODEL), F32),
        grid=(t // TM_MIX,),
        in_specs=[
            pl.BlockSpec((TM_MIX, D_MODEL), lambda i: (i, 0)),
            pl.BlockSpec((TM_MIX, D_MODEL), lambda i: (i, 0)),
            group_spec(G_B), group_spec(G_C), group_spec(G_U),
            group_spec(G_GA), group_spec(G_GB),
            halo_spec(G_B), halo_spec(G_U),
            pl.BlockSpec((CONV_K, D_MODEL), lambda i: (0, 0)),
            weight_spec(), weight_spec(), weight_spec(),
        ],
        out_specs=pl.BlockSpec((TM_MIX, D_MODEL), lambda i: (i, 0)),
        compiler_params=pltpu.CompilerParams(
            dimension_semantics=("arbitrary",),
            vmem_limit_bytes=VMEM_LIMIT),
        name="mixer_out",
    )(x, o_gated, proj, proj, proj, proj, proj, proj, proj, conv_w, wph, wpc, wo)


def _dense_ffn_kernel(x_ref, nw_ref, w1_ref, w3_ref, w2_ref, o_ref):
    x = x_ref[...]
    h = _rms_norm_rows(x, nw_ref[...]).astype(BF16)
    a = jnp.dot(h, w1_ref[...], preferred_element_type=F32)
    b = jnp.dot(h, w3_ref[...], preferred_element_type=F32)
    act = (_silu(a) * b).astype(BF16)
    o_ref[...] = x + jnp.dot(act, w2_ref[...], preferred_element_type=F32)


def _dense_ffn(x, norm_w, w1, w3, w2):
    t = x.shape[0]
    return pl.pallas_call(
        _dense_ffn_kernel,
        out_shape=jax.ShapeDtypeStruct((t, D_MODEL), F32),
        grid=(t // TM_FFN,),
        in_specs=[
            pl.BlockSpec((TM_FFN, D_MODEL), lambda i: (i, 0)),
            pl.BlockSpec((1, D_MODEL), lambda i: (0, 0)),
            pl.BlockSpec((D_MODEL, D_FF), lambda i: (0, 0)),
            pl.BlockSpec((D_MODEL, D_FF), lambda i: (0, 0)),
            pl.BlockSpec((D_FF, D_MODEL), lambda i: (0, 0)),
        ],
        out_specs=pl.BlockSpec((TM_FFN, D_MODEL), lambda i: (i, 0)),
        compiler_params=pltpu.CompilerParams(
            dimension_semantics=("arbitrary",),
            vmem_limit_bytes=VMEM_LIMIT),
        name="dense_ffn",
    )(x, norm_w, w1, w3, w2)


def _route_kernel(x_ref, nw_ref, rwt_ref, rwp_ref, triu_ref,
                  h_ref, meta_ref, wts_ref, cnt_ref, carry_scr):
    @pl.when(pl.program_id(0) == 0)
    def _():
        carry_scr[...] = jnp.zeros_like(carry_scr)

    h = _rms_norm_rows(x_ref[...], nw_ref[...])
    h_ref[...] = h
    h_hi, h_lo = _split_bf16(h)

    rwt_hi, rwt_lo = _split_bf16(rwt_ref[...])
    lt = _dot3(rwt_hi, rwt_lo, h_hi, h_lo, (((1,), (1,)), ((), ())))
    sub = lax.broadcasted_iota(I32, lt.shape, 0)
    m1 = jnp.max(lt, axis=0, keepdims=True)
    i1 = jnp.min(jnp.where(lt == m1, sub, N_EXPERTS), axis=0, keepdims=True)
    rest = jnp.where(sub == i1, -jnp.inf, lt)
    m2 = jnp.max(rest, axis=0, keepdims=True)
    i2 = jnp.min(jnp.where(rest == m2, sub, N_EXPERTS), axis=0, keepdims=True)
    oh1 = sub == i1
    oh2 = sub == i2
    member = jnp.where(jnp.logical_or(oh1, oh2), 1.0, 0.0)
    prefix = jnp.dot(member.astype(BF16), triu_ref[...], preferred_element_type=F32)
    carry = carry_scr[...]
    base = carry[:, 0:1] + prefix
    rank1 = jnp.sum(jnp.where(oh1, base, 0.0), axis=0, keepdims=True)
    rank2 = jnp.sum(jnp.where(oh2, base, 0.0), axis=0, keepdims=True)
    carry_new = carry + jnp.sum(member, axis=1, keepdims=True)
    carry_scr[...] = carry_new
    cnt_ref[...] = carry_new.astype(I32)
    meta_ref[...] = jnp.where(
        sub == 0, i1, jnp.where(
            sub == 1, i2, jnp.where(
                sub == 2, rank1.astype(I32), jnp.where(
                    sub == 3, rank2.astype(I32), 0))))

    rwp_hi, rwp_lo = _split_bf16(rwp_ref[...])
    lr = _dot3(h_hi, h_lo, rwp_hi, rwp_lo, (((1,), (0,)), ((), ())))
    lane = lax.broadcasted_iota(I32, lr.shape, 1)
    lr = jnp.where(lane < N_EXPERTS, lr, -jnp.inf)
    m1r = jnp.max(lr, axis=-1, keepdims=True)
    i1r = jnp.min(jnp.where(lr == m1r, lane, LANES), axis=-1, keepdims=True)
    m2r = jnp.max(jnp.where(lane == i1r, -jnp.inf, lr), axis=-1, keepdims=True)
    e2 = jnp.exp(m2r - m1r)
    w1 = 1.0 / (1.0 + e2)
    w2 = e2 * w1
    wts_ref[...] = jnp.where(lane == 0, w1, jnp.where(lane == 1, w2, 0.0))


def _route(x, norm_w, router_w):
    t = x.shape[0]
    rwt = router_w.T
    rwp = jnp.pad(router_w, ((0, 0), (0, LANES - N_EXPERTS)))
    triu = jnp.triu(jnp.ones((TM_ROUTE, TM_ROUTE), BF16), k=1)
    return pl.pallas_call(
        _route_kernel,
        out_shape=(
            jax.ShapeDtypeStruct((t, D_MODEL), F32),
            jax.ShapeDtypeStruct((SUBLANES, t), I32),
            jax.ShapeDtypeStruct((t, LANES), F32),
            jax.ShapeDtypeStruct((N_EXPERTS, LANES), I32),
        ),
        grid=(t // TM_ROUTE,),
        in_specs=[
            pl.BlockSpec((TM_ROUTE, D_MODEL), lambda i: (i, 0)),
            pl.BlockSpec((1, D_MODEL), lambda i: (0, 0)),
            pl.BlockSpec((N_EXPERTS, D_MODEL), lambda i: (0, 0)),
            pl.BlockSpec((D_MODEL, LANES), lambda i: (0, 0)),
            pl.BlockSpec((TM_ROUTE, TM_ROUTE), lambda i: (0, 0)),
        ],
        out_specs=(
            pl.BlockSpec((TM_ROUTE, D_MODEL), lambda i: (i, 0)),
            pl.BlockSpec((SUBLANES, TM_ROUTE), lambda i: (0, i)),
            pl.BlockSpec((TM_ROUTE, LANES), lambda i: (i, 0)),
            pl.BlockSpec((N_EXPERTS, LANES), lambda i: (0, 0)),
        ),
        scratch_shapes=[pltpu.VMEM((N_EXPERTS, LANES), F32)],
        compiler_params=pltpu.CompilerParams(
            dimension_semantics=("arbitrary",),
            vmem_limit_bytes=VMEM_LIMIT),
        name="moe_route",
    )(x, norm_w, rwt, rwp, triu)


def _store_rows_as_tiles(ref, value):
    n = value.shape[0]
    for c in range(ROW_TILE):
        ref[pl.ds(c, n, stride=ROW_TILE), :] = value[:, c * LANES:(c + 1) * LANES]


def _load_tile_chunk(ref, c, n):
    return ref[pl.ds(c, n, stride=ROW_TILE), :]


def _pos_copy(pos_hbm, pos_smem, sem, step, slot):
    n = pos_hbm.shape[1]
    dst = pos_smem.at[pl.ds(pl.multiple_of(slot * n, n), n)]
    return pltpu.make_async_copy(pos_hbm.at[step], dst, sem.at[slot])


def _dispatch_kernel(zt_ref, pos_hbm, h_ref, xs_hbm, pos_smem, hbuf, zbuf, psem, dsem, zsem):
    i = pl.program_id(0)
    n = pl.num_programs(0)
    slot = i % 2

    @pl.when(i == 0)
    def _():
        _pos_copy(pos_hbm, pos_smem, psem, 0, 0).start()

    _pos_copy(pos_hbm, pos_smem, psem, i, slot).wait()

    @pl.when(i + 1 < n)
    def _():
        _pos_copy(pos_hbm, pos_smem, psem, i + 1, 1 - slot).start()

    def wait_rows_from(s):
        for _ in range(TOP_K):
            pltpu.make_async_copy(hbuf.at[s], xs_hbm.at[pl.ds(0, TM_DISP * ROW_TILE)],
                                  dsem.at[s]).wait()

    @pl.when(i == 0)
    def _():
        zbuf[...] = jnp.zeros_like(zbuf)
        n_tiles = xs_hbm.shape[0] // (TM_GRP * ROW_TILE)

        def zero_tile_copy(k):
            r0 = pl.multiple_of(k * (TM_GRP * ROW_TILE), TM_GRP * ROW_TILE)
            return pltpu.make_async_copy(zbuf, xs_hbm.at[pl.ds(r0, TM_GRP * ROW_TILE)], zsem)

        def for_each_zero_tile(fn):
            for e in range(N_EXPERTS):
                @pl.when(zt_ref[N_EXPERTS + e] > 0)
                def _():
                    fn(zero_tile_copy(zt_ref[e]))

                @pl.when(zt_ref[2 * N_EXPERTS] + e < n_tiles)
                def _():
                    fn(zero_tile_copy(zt_ref[2 * N_EXPERTS] + e))

        for_each_zero_tile(lambda cp: cp.start())
        for_each_zero_tile(lambda cp: cp.wait())

    @pl.when(i >= 2)
    def _():
        wait_rows_from(slot)

    _store_rows_as_tiles(hbuf.at[slot], h_ref[...])

    pos_base = slot * (TOP_K * TM_DISP)

    def issue(t, carry):
        src = hbuf.at[slot, pl.ds(pl.multiple_of(t * ROW_TILE, ROW_TILE), ROW_TILE)]
        for j in range(TOP_K):
            p = pl.multiple_of(pos_smem[pos_base + j * TM_DISP + t], ROW_TILE)
            pltpu.make_async_copy(src, xs_hbm.at[pl.ds(p, ROW_TILE)],
                                  dsem.at[slot]).start(priority=j)
        return carry

    lax.fori_loop(0, TM_DISP, issue, 0, unroll=DMA_ISSUE_UNROLL)

    @pl.when(i == n - 1)
    def _():
        wait_rows_from(slot)

        @pl.when(n > 1)
        def _():
            wait_rows_from(1 - slot)


def _dispatch(h, pos_tiles, zero_tiles, n_sorted_rows):
    t = h.shape[0]
    return pl.pallas_call(
        _dispatch_kernel,
        out_shape=jax.ShapeDtypeStruct((n_sorted_rows * ROW_TILE, LANES), F32),
        grid_spec=pltpu.PrefetchScalarGridSpec(
            num_scalar_prefetch=1,
            grid=(t // TM_DISP,),
            in_specs=[pl.BlockSpec(memory_space=pl.ANY),
                      pl.BlockSpec((TM_DISP, D_MODEL), lambda i, zt: (i, 0))],
            out_specs=pl.BlockSpec(memory_space=pl.ANY),
            scratch_shapes=[
                pltpu.SMEM((2 * TOP_K * TM_DISP,), I32),
                pltpu.VMEM((2, TM_DISP * ROW_TILE, LANES), F32),
                pltpu.VMEM((TM_GRP * ROW_TILE, LANES), F32),
                pltpu.SemaphoreType.DMA((2,)),
                pltpu.SemaphoreType.DMA((2,)),
                pltpu.SemaphoreType.DMA(()),
            ]),
        compiler_params=pltpu.CompilerParams(
            dimension_semantics=("arbitrary",),
            vmem_limit_bytes=VMEM_LIMIT),
        name="moe_dispatch",
    )(zero_tiles, pos_tiles, h)


def _experts_kernel(te_ref, nv_ref, xs_ref, w1_ref, w3_ref, w2_ref, y_ref, xb_scr):
    k = pl.program_id(0)

    @pl.when(k < nv_ref[0])
    def _():
        for c in range(ROW_TILE):
            xb_scr[:, c * LANES:(c + 1) * LANES] = _load_tile_chunk(
                xs_ref, c, TM_GRP).astype(BF16)
        xb = xb_scr[...]
        y = None
        for lo, hi in zip(FF_SPLITS[:-1], FF_SPLITS[1:]):
            cols = slice(lo, hi)
            a = jnp.dot(xb, w1_ref[0, :, cols], preferred_element_type=F32)
            b = jnp.dot(xb, w3_ref[0, :, cols], preferred_element_type=F32)
            act = (_silu(a) * b).astype(BF16)
            part = jnp.dot(act, w2_ref[0, cols, :], preferred_element_type=F32)
            y = part if y is None else y + part
        _store_rows_as_tiles(y_ref, y)

    @pl.when(k >= nv_ref[0])
    def _():
        y_ref[...] = jnp.zeros_like(y_ref)


def _experts(xs, tile_expert, n_valid, w1, w3, w2):
    n_rows = xs.shape[0] // ROW_TILE

    def row_map(k, te, nv):
        return (jnp.minimum(k, nv[0] - 1), 0)

    def out_map(k, te, nv):
        return (k, 0)

    def w_map(k, te, nv):
        return (te[jnp.minimum(k, nv[0] - 1)], 0, 0)

    def weight_spec(shape):
        return pl.BlockSpec((1,) + shape, w_map, pipeline_mode=pl.Buffered(1))

    return pl.pallas_call(
        _experts_kernel,
        out_shape=jax.ShapeDtypeStruct((n_rows * ROW_TILE, LANES), F32),
        grid_spec=pltpu.PrefetchScalarGridSpec(
            num_scalar_prefetch=2,
            grid=(n_rows // TM_GRP,),
            in_specs=[
                pl.BlockSpec((TM_GRP * ROW_TILE, LANES), row_map),
                weight_spec((D_MODEL, D_FF)),
                weight_spec((D_MODEL, D_FF)),
                weight_spec((D_FF, D_MODEL)),
            ],
            out_specs=pl.BlockSpec((TM_GRP * ROW_TILE, LANES), out_map),
            scratch_shapes=[pltpu.VMEM((TM_GRP, D_MODEL), BF16)]),
        compiler_params=pltpu.CompilerParams(
            dimension_semantics=("arbitrary",),
            vmem_limit_bytes=VMEM_LIMIT),
        name="moe_experts",
    )(tile_expert, n_valid, xs, w1, w3, w2)


def _combine_kernel(fuse_norm, pos_hbm, y_hbm, x_ref, wts_ref, fnw_ref, o_ref,
                    pos_smem, ybuf, psem, gsem):
    i = pl.program_id(0)
    n = pl.num_programs(0)
    slot = i % 2

    def issue_gathers(s):
        pos_base = s * (TOP_K * TM_COMB)

        def body(t, carry):
            dst_rows = pl.ds(pl.multiple_of(t * ROW_TILE, ROW_TILE), ROW_TILE)
            for j in range(TOP_K):
                p = pl.multiple_of(pos_smem[pos_base + j * TM_COMB + t], ROW_TILE)
                pltpu.make_async_copy(y_hbm.at[pl.ds(p, ROW_TILE)],
                                      ybuf.at[s, j, dst_rows], gsem.at[s]).start(priority=j)
            return carry

        lax.fori_loop(0, TM_COMB, body, 0, unroll=DMA_ISSUE_UNROLL)

    @pl.when(i == 0)
    def _():
        first = _pos_copy(pos_hbm, pos_smem, psem, 0, 0)
        first.start()
        first.wait()
        issue_gathers(0)

        @pl.when(n > 1)
        def _():
            _pos_copy(pos_hbm, pos_smem, psem, 1, 1).start()

    @pl.when(i + 1 < n)
    def _():
        _pos_copy(pos_hbm, pos_smem, psem, i + 1, 1 - slot).wait()
        issue_gathers(1 - slot)

        @pl.when(i + 2 < n)
        def _():
            _pos_copy(pos_hbm, pos_smem, psem, i + 2, slot).start()

    for j in range(TOP_K):
        pltpu.make_async_copy(y_hbm.at[pl.ds(0, TM_COMB * ROW_TILE)], ybuf.at[slot, j],
                              gsem.at[slot]).wait()

    w = wts_ref[...]
    w1 = w[:, 0:1]
    w2 = w[:, 1:2]
    chunks = []
    for c in range(ROW_TILE):
        chunks.append(x_ref[:, c * LANES:(c + 1) * LANES]
                      + w1 * _load_tile_chunk(ybuf.at[slot, 0], c, TM_COMB)
                      + w2 * _load_tile_chunk(ybuf.at[slot, 1], c, TM_COMB))
    if fuse_norm:
        ss = chunks[0] * chunks[0]
        for ch in chunks[1:]:
            ss = ss + ch * ch
        inv = lax.rsqrt(jnp.sum(ss, axis=-1, keepdims=True) * (1.0 / D_MODEL) + EPS)
        fnw = fnw_ref[...]
        chunks = [ch * inv * fnw[:, c * LANES:(c + 1) * LANES] for c, ch in enumerate(chunks)]
    for c, ch in enumerate(chunks):
        o_ref[:, c * LANES:(c + 1) * LANES] = ch


def _combine(x, y, pos_tiles, wts, final_norm_w, fuse_norm):
    t = x.shape[0]
    return pl.pallas_call(
        functools.partial(_combine_kernel, fuse_norm),
        out_shape=jax.ShapeDtypeStruct((t, D_MODEL), F32),
        grid=(t // TM_COMB,),
        in_specs=[
            pl.BlockSpec(memory_space=pl.ANY),
            pl.BlockSpec(memory_space=pl.ANY),
            pl.BlockSpec((TM_COMB, D_MODEL), lambda i: (i, 0)),
            pl.BlockSpec((TM_COMB, LANES), lambda i: (i, 0)),
            pl.BlockSpec((1, D_MODEL), lambda i: (0, 0)),
        ],
        out_specs=pl.BlockSpec((TM_COMB, D_MODEL), lambda i: (i, 0)),
        scratch_shapes=[
            pltpu.SMEM((2 * TOP_K * TM_COMB,), I32),
            pltpu.VMEM((2, TOP_K, TM_COMB * ROW_TILE, LANES), F32),
            pltpu.SemaphoreType.DMA((2,)),
            pltpu.SemaphoreType.DMA((2,)),
        ],
        compiler_params=pltpu.CompilerParams(dimension_semantics=("arbitrary",)),
        name="moe_combine",
    )(pos_tiles, y, x, wts, final_norm_w)


def _tile_major(pos, tile):
    t = pos.shape[1]
    return pos.reshape(TOP_K, t // tile, tile).transpose(1, 0, 2).reshape(t // tile, TOP_K * tile)


def _moe_ffn(x, norm_w, router_w, w1, w3, w2, final_norm_w, fuse_norm):
    t = x.shape[0]
    h, meta, wts, cnt = _route(x, norm_w, router_w)

    counts = cnt[:, 0]
    padded = ((counts + TM_GRP - 1) // TM_GRP) * TM_GRP
    ends = jnp.cumsum(padded)
    offs = ends - padded
    n_sorted_rows = TOP_K * t + N_EXPERTS * TM_GRP
    tile_start = jnp.arange(n_sorted_rows // TM_GRP, dtype=I32) * TM_GRP
    tile_expert = jnp.minimum(
        jnp.sum((tile_start[:, None] >= ends[None, :]).astype(I32), axis=1), N_EXPERTS - 1)
    n_valid = (ends[-1] // TM_GRP).astype(I32).reshape(1)
    pos = (jnp.stack([offs[meta[0]] + meta[2], offs[meta[1]] + meta[3]]) * ROW_TILE).astype(I32)
    zero_tiles = jnp.concatenate([ends // TM_GRP - 1, padded, n_valid]).astype(I32)

    xs = _dispatch(h, _tile_major(pos, TM_DISP), zero_tiles, n_sorted_rows)
    y = _experts(xs, tile_expert.astype(I32), n_valid, w1, w3, w2)
    return _combine(x, y, _tile_major(pos, TM_COMB), wts, final_norm_w, fuse_norm)


def _final_norm_kernel(x_ref, nw_ref, o_ref):
    o_ref[...] = _rms_norm_rows(x_ref[...], nw_ref[...])


def _final_norm(x, norm_w):
    t = x.shape[0]
    return pl.pallas_call(
        _final_norm_kernel,
        out_shape=jax.ShapeDtypeStruct((t, D_MODEL), F32),
        grid=(t // TM_PROJ,),
        in_specs=[
            pl.BlockSpec((TM_PROJ, D_MODEL), lambda i: (i, 0)),
            pl.BlockSpec((1, D_MODEL), lambda i: (0, 0)),
        ],
        out_specs=pl.BlockSpec((TM_PROJ, D_MODEL), lambda i: (i, 0)),
        compiler_params=pltpu.CompilerParams(dimension_semantics=("arbitrary",)),
        name="final_norm",
    )(x, norm_w)


def kernel(x, w_in, lower_bounds, hgrn_norm_w, conv_w, w_proj_hgrn, w_proj_conv, w_out,
           norm_mix, norm_ffn, dense_w1, dense_w3, dense_w2, router_w,
           expert_w1, expert_w3, expert_w2, final_norm):
    batch, seq, d = x.shape
    depth = w_in.shape[0]
    assert d == D_MODEL and seq % CT_REC == 0 and (batch * seq) % TM_PROJ == 0
    assert w_in.shape[2] == N_GROUPS * D_MODEL

    xt = x.reshape(batch * seq, d)
    tri = jnp.tril(jnp.ones((CHUNK, CHUNK), F32))
    lower_bounds = lower_bounds.astype(F32)
    final_w = final_norm.reshape(1, d)
    normed = False

    for layer in range(depth):
        proj = _inproj(xt, norm_mix[layer].reshape(1, d), w_in[layer].astype(BF16))
        o_gated = _recurrence(proj, lower_bounds, hgrn_norm_w[layer].reshape(1, HEAD_DIM),
                              tri, layer, batch, seq)
        xt = _mixout(xt, o_gated, proj, conv_w[layer],
                     w_proj_hgrn[layer].astype(BF16), w_proj_conv[layer].astype(BF16),
                     w_out[layer].astype(BF16), seq)
        j = layer // 2
        nw = norm_ffn[layer].reshape(1, d)
        if layer % 2 == 0:
            xt = _dense_ffn(xt, nw, dense_w1[j].astype(BF16), dense_w3[j].astype(BF16),
                            dense_w2[j].astype(BF16))
        else:
            normed = layer == depth - 1
            xt = _moe_ffn(xt, nw, router_w[j].astype(F32), expert_w1[j].astype(BF16),
                          expert_w3[j].astype(BF16), expert_w2[j].astype(BF16),
                          final_w, normed)
    out = xt if normed else _final_norm(xt, final_w)
    return out.reshape(batch, seq, d)
```

```python
import functools

import jax
import jax.numpy as jnp
from jax import lax
from jax.experimental import pallas as pl
from jax.experimental.pallas import tpu as pltpu

D_MODEL = 1024
HEADS = 8
HEAD_DIM = 128
HGRN_SCALE = HEAD_DIM ** -0.5
F_MIN = 1e-6
CONV_K = 3
D_FF = 2816
N_EXPERTS = 8
TOP_K = 2
EPS = 1e-6
LOG2_E = 1.4426950408889634
N_GROUPS = 9
G_Q, G_F, G_I, G_G, G_B, G_C, G_U, G_GA, G_GB = range(N_GROUPS)

LANES = 128
SUBLANES = 8
ROW_TILE = SUBLANES
BF16_SUBLANES = 16
VMEM_LIMIT = 56 * 1024 * 1024

TM_PROJ = 512
TN_PROJ = 2304
TM_MIX = 512
TM_FFN = 512
MXU_TILE = 256
FF_SPLITS = (0, 6 * MXU_TILE, D_FF)
TM_ROUTE = 512
TM_DISP = 512
TM_GRP = 512
TM_COMB = 512
DMA_ISSUE_UNROLL = 8
CT_REC = 1024
CHUNK = 128
SUB = 32
N_SUB = CHUNK // SUB
MAX_EXP2 = 126.0
REC_UNROLL = 4

F32 = jnp.float32
BF16 = jnp.bfloat16
I32 = jnp.int32


def _rms_norm_rows(x, w):
    ms = jnp.mean(x * x, axis=-1, keepdims=True)
    return x * lax.rsqrt(ms + EPS) * w


def _exp_neg(x):
    return jnp.exp2(x * (-LOG2_E))


def _sigmoid(x):
    return 1.0 / (1.0 + _exp_neg(x))


def _silu(x):
    return x * _sigmoid(x)


def _split_bf16(v):
    hi = v.astype(BF16)
    lo = (v - hi.astype(F32)).astype(BF16)
    return hi, lo


def _dot3(a_hi, a_lo, b_hi, b_lo, dims):
    dg = functools.partial(lax.dot_general, dimension_numbers=dims,
                           preferred_element_type=F32)
    return dg(a_hi, b_hi) + dg(a_hi, b_lo) + dg(a_lo, b_hi)


def _inproj_kernel(x_ref, nw_ref, w_ref, o_ref):
    h = _rms_norm_rows(x_ref[...], nw_ref[...]).astype(BF16)
    for c in range(w_ref.shape[1] // TN_PROJ):
        cols = slice(c * TN_PROJ, (c + 1) * TN_PROJ)
        o_ref[:, cols] = jnp.dot(h, w_ref[:, cols],
                                 preferred_element_type=F32).astype(o_ref.dtype)


def _inproj(x, norm_w, w_in_bf16):
    t = x.shape[0]
    n = w_in_bf16.shape[1]
    assert n % TN_PROJ == 0
    return pl.pallas_call(
        _inproj_kernel,
        out_shape=jax.ShapeDtypeStruct((t, n), BF16),
        grid=(t // TM_PROJ,),
        in_specs=[
            pl.BlockSpec((TM_PROJ, D_MODEL), lambda i: (i, 0)),
            pl.BlockSpec((1, D_MODEL), lambda i: (0, 0)),
            pl.BlockSpec((D_MODEL, n), lambda i: (0, 0), pipeline_mode=pl.Buffered(1)),
        ],
        out_specs=pl.BlockSpec((TM_PROJ, n), lambda i: (i, 0)),
        compiler_params=pltpu.CompilerParams(
            dimension_semantics=("arbitrary",),
            vmem_limit_bytes=VMEM_LIMIT),
        name="inproj",
    )(x, norm_w, w_in_bf16)


def _layer_lower_bound(lb_raw, layer):
    m = jnp.max(lb_raw, axis=0, keepdims=True)
    e = jnp.exp(lb_raw - m)
    soft = e / jnp.sum(e, axis=0, keepdims=True)
    acc = soft[0:1, :]
    for j in range(1, layer + 1):
        acc = acc + soft[j:j + 1, :]
    return acc - soft[0:1, :]


def _rec_chunk(rows, slot, q_ref, f_ref, i_ref, g_ref, o_ref, st_ref, lb, gw, tri_f32, tri_bf,
               lhs_s, rhs_s, kst_s, sg_s):
    z = f_ref[rows, :].astype(F32)
    e = _exp_neg(jnp.abs(z))
    r = 1.0 / (1.0 + e)
    er = e * r
    pos = z >= 0.0
    sig_pos = jnp.where(pos, r, er)
    sig_neg = jnp.where(pos, er, r)
    one_m_lb = 1.0 - lb
    logf = jnp.log2(jnp.maximum(lb + one_m_lb * sig_pos, F_MIN))
    kk = one_m_lb * sig_neg

    hi, lo = _split_bf16(logf)
    cum2 = jnp.dot(tri_bf, jnp.concatenate([hi, lo], axis=1), preferred_element_type=F32)
    cum = cum2[:, :D_MODEL] + cum2[:, D_MODEL:]

    s_ref = [jnp.zeros((1, D_MODEL), F32)]
    for j in range(1, N_SUB):
        s_ref.append(cum[j * SUB - 1:j * SUB, :])
    cum_end = cum[CHUNK - 1:CHUNK, :]
    decay_end = jnp.exp2(cum_end)

    qh = _silu(q_ref[rows, :].astype(F32))
    for j in range(N_SUB):
        blk = slice(j * SUB, (j + 1) * SUB)
        q_loc = qh[blk] * jnp.exp2(cum[blk] - s_ref[j])
        k_loc = kk[blk] * jnp.exp2(jnp.minimum(s_ref[j] - cum[blk], MAX_EXP2))
        rhs_s[slot, blk, j * D_MODEL:(j + 1) * D_MODEL] = k_loc.astype(BF16)
        kst_s[slot, blk, :] = (k_loc * jnp.exp2(cum_end - s_ref[j])).astype(BF16)
        for jj in range(j + 1):
            val = q_loc if jj == j else q_loc * jnp.exp2(s_ref[j] - s_ref[jj])
            lhs_s[slot, blk, jj * D_MODEL:(jj + 1) * D_MODEL] = val.astype(BF16)
    sg_s[slot] = _silu(g_ref[rows, :].astype(F32))

    for h in range(HEADS):
        cols = slice(h * HEAD_DIM, (h + 1) * HEAD_DIM)
        seg_cols = [slice(j * D_MODEL + h * HEAD_DIM, j * D_MODEL + (h + 1) * HEAD_DIM)
                    for j in range(N_SUB)]
        lhs = jnp.concatenate([lhs_s[slot, :, sc] for sc in seg_cols], axis=1)
        rhs = jnp.concatenate([rhs_s[slot, :, sc] for sc in seg_cols], axis=1)
        scores = lax.dot_general(lhs, rhs, (((1,), (1,)), ((), ())),
                                 preferred_element_type=F32) * tri_f32
        v_bf = i_ref[rows, cols]
        st = st_ref[h]
        o = (jnp.dot(scores.astype(BF16), v_bf, preferred_element_type=F32)
             + lax.dot_general(lhs_s[slot, :, seg_cols[0]], st.astype(BF16),
                               (((1,), (1,)), ((), ())), preferred_element_type=F32))
        st_ref[h] = st * decay_end[:, cols] + lax.dot_general(
            v_bf, kst_s[slot, :, cols], (((0,), (0,)), ((), ())), preferred_element_type=F32)
        ms = jnp.mean(o * o, axis=-1, keepdims=True)
        out = o * lax.rsqrt(ms + EPS / (HGRN_SCALE * HGRN_SCALE)) * gw * sg_s[slot, :, cols]
        o_ref[rows, cols] = out.astype(o_ref.dtype)


def _rec_kernel(layer, q_ref, f_ref, i_ref, g_ref, lbraw_ref, gw_ref, tri_ref,
                 o_ref, st_ref, lhs_s, rhs_s, kst_s, sg_s):
    @pl.when(pl.program_id(1) == 0)
    def _():
        st_ref[...] = jnp.zeros_like(st_ref)
        lhs_s[...] = jnp.zeros_like(lhs_s)
        rhs_s[...] = jnp.zeros_like(rhs_s)

    lb = _layer_lower_bound(lbraw_ref[...], layer)
    gw = gw_ref[...]
    tri_f32 = tri_ref[...]
    tri_bf = tri_f32.astype(BF16)

    def body(cp, carry):
        for u in range(REC_UNROLL):
            r0 = pl.multiple_of((cp * REC_UNROLL + u) * CHUNK, CHUNK)
            _rec_chunk(pl.ds(r0, CHUNK), u, q_ref, f_ref, i_ref, g_ref, o_ref, st_ref,
                       lb, gw, tri_f32, tri_bf, lhs_s, rhs_s, kst_s, sg_s)
        return carry

    lax.fori_loop(0, CT_REC // (CHUNK * REC_UNROLL), body, 0)


def _recurrence(proj, lower_bounds, gnorm_w, tri, layer, batch, seq):
    t = proj.shape[0]
    steps = seq // CT_REC

    def group_spec(g):
        return pl.BlockSpec((CT_REC, D_MODEL), lambda b, s: (b * steps + s, g))

    return pl.pallas_call(
        functools.partial(_rec_kernel, layer),
        out_shape=jax.ShapeDtypeStruct((t, D_MODEL), BF16),
        grid=(batch, steps),
        in_specs=[
            group_spec(G_Q), group_spec(G_F), group_spec(G_I), group_spec(G_G),
            pl.BlockSpec(lower_bounds.shape, lambda b, s: (0, 0)),
            pl.BlockSpec((1, HEAD_DIM), lambda b, s: (0, 0)),
            pl.BlockSpec((CHUNK, CHUNK), lambda b, s: (0, 0)),
        ],
        out_specs=pl.BlockSpec((CT_REC, D_MODEL), lambda b, s: (b * steps + s, 0)),
        scratch_shapes=[
            pltpu.VMEM((HEADS, HEAD_DIM, HEAD_DIM), F32),
            pltpu.VMEM((REC_UNROLL, CHUNK, N_SUB * D_MODEL), BF16),
            pltpu.VMEM((REC_UNROLL, CHUNK, N_SUB * D_MODEL), BF16),
            pltpu.VMEM((REC_UNROLL, CHUNK, D_MODEL), BF16),
            pltpu.VMEM((REC_UNROLL, CHUNK, D_MODEL), F32),
        ],
        compiler_params=pltpu.CompilerParams(
            dimension_semantics=("arbitrary", "arbitrary"),
            vmem_limit_bytes=VMEM_LIMIT),
        name="hgrn2_recurrence",
    )(proj, proj, proj, proj, lower_bounds, gnorm_w, tri)


def _mixout_kernel(seq, x_ref, oa_ref, b_ref, c_ref, u_ref, ga_ref, gb_ref,
                   bh_ref, uh_ref, cw_ref, wph_ref, wpc_ref, wo_ref, o_ref):
    i = pl.program_id(0)
    v = b_ref[...].astype(F32) * u_ref[...].astype(F32)
    at_seq_start = (i * TM_MIX) % seq == 0
    halo = bh_ref[...].astype(F32) * uh_ref[...].astype(F32)
    halo = jnp.where(at_seq_start, 0.0, halo)
    prev1 = halo[BF16_SUBLANES - 1:BF16_SUBLANES, :]
    prev2 = halo[BF16_SUBLANES - 2:BF16_SUBLANES - 1, :]
    row = lax.broadcasted_iota(I32, v.shape, 0)
    v1 = jnp.where(row == 0, prev1, pltpu.roll(v, 1, 0))
    v2 = jnp.where(row == 0, prev2,
                   jnp.where(row == 1, prev1, pltpu.roll(v, 2, 0)))
    cw = cw_ref[...]
    conv = cw[0:1, :] * v2 + cw[1:2, :] * v1 + cw[2:3, :] * v
    yb_in = (c_ref[...].astype(F32) * conv).astype(BF16)
    y_b = jnp.dot(yb_in, wpc_ref[...], preferred_element_type=F32)
    y_a = jnp.dot(oa_ref[...], wph_ref[...], preferred_element_type=F32)
    merged = (_sigmoid(ga_ref[...].astype(F32)) * y_a
              + _sigmoid(gb_ref[...].astype(F32)) * y_b).astype(BF16)
    o_ref[...] = x_ref[...] + jnp.dot(merged, wo_ref[...], preferred_element_type=F32)


def _mixout(x, o_gated, proj, conv_w, wph, wpc, wo, seq):
    t = x.shape[0]
    halo_per_tile = TM_MIX // BF16_SUBLANES

    def group_spec(g):
        return pl.BlockSpec((TM_MIX, D_MODEL), lambda i: (i, g))

    def halo_spec(g):
        return pl.BlockSpec(
            (BF16_SUBLANES, D_MODEL),
            lambda i: (jnp.maximum(i * halo_per_tile - 1, 0), g))

    def weight_spec():
        return pl.BlockSpec((D_MODEL, D_MODEL), lambda i: (0, 0))

    return pl.pallas_call(
        functools.partial(_mixout_kernel, seq),
        out_shape=jax.ShapeDtypeStruct((t, D_MODEL), F32),
        grid=(t // TM_MIX,),
        in_specs=[
            pl.BlockSpec((TM_MIX, D_MODEL), lambda i: (i, 0)),
            pl.BlockSpec((TM_MIX, D_MODEL), lambda i: (i, 0)),
            group_spec(G_B), group_spec(G_C), group_spec(G_U),
            group_spec(G_GA), group_spec(G_GB),
            halo_spec(G_B), halo_spec(G_U),
            pl.BlockSpec((CONV_K, D_MODEL), lambda i: (0, 0)),
            weight_spec(), weight_spec(), weight_spec(),
        ],
        out_specs=pl.BlockSpec((TM_MIX, D_MODEL), lambda i: (i, 0)),
        compiler_params=pltpu.CompilerParams(
            dimension_semantics=("arbitrary",),
            vmem_limit_bytes=VMEM_LIMIT),
        name="mixer_out",
    )(x, o_gated, proj, proj, proj, proj, proj, proj, proj, conv_w, wph, wpc, wo)


def _dense_ffn_kernel(x_ref, nw_ref, w1_ref, w3_ref, w2_ref, o_ref):
    x = x_ref[...]
    h = _rms_norm_rows(x, nw_ref[...]).astype(BF16)
    a = jnp.dot(h, w1_ref[...], preferred_element_type=F32)
    b = jnp.dot(h, w3_ref[...], preferred_element_type=F32)
    act = (_silu(a) * b).astype(BF16)
    o_ref[...] = x + jnp.dot(act, w2_ref[...], preferred_element_type=F32)


def _dense_ffn(x, norm_w, w1, w3, w2):
    t = x.shape[0]
    return pl.pallas_call(
        _dense_ffn_kernel,
        out_shape=jax.ShapeDtypeStruct((t, D_MODEL), F32),
        grid=(t // TM_FFN,),
        in_specs=[
            pl.BlockSpec((TM_FFN, D_MODEL), lambda i: (i, 0)),
            pl.BlockSpec((1, D_MODEL), lambda i: (0, 0)),
            pl.BlockSpec((D_MODEL, D_FF), lambda i: (0, 0)),
            pl.BlockSpec((D_MODEL, D_FF), lambda i: (0, 0)),
            pl.BlockSpec((D_FF, D_MODEL), lambda i: (0, 0)),
        ],
        out_specs=pl.BlockSpec((TM_FFN, D_MODEL), lambda i: (i, 0)),
        compiler_params=pltpu.CompilerParams(
            dimension_semantics=("arbitrary",),
            vmem_limit_bytes=VMEM_LIMIT),
        name="dense_ffn",
    )(x, norm_w, w1, w3, w2)


def _route_kernel(x_ref, nw_ref, rwp_ref, triu_ref,
                  h_ref, meta_ref, wts_ref, cnt_ref, carry_scr):
    @pl.when(pl.program_id(0) == 0)
    def _():
        carry_scr[...] = jnp.zeros_like(carry_scr)

    h = _rms_norm_rows(x_ref[...], nw_ref[...])
    h_ref[...] = h
    h_hi, h_lo = _split_bf16(h)
    rwp_hi, rwp_lo = _split_bf16(rwp_ref[...])
    lr = _dot3(h_hi, h_lo, rwp_hi, rwp_lo, (((1,), (0,)), ((), ())))

    lt = jnp.transpose(lr)[:N_EXPERTS, :]
    sub = lax.broadcasted_iota(I32, lt.shape, 0)
    m1 = jnp.max(lt, axis=0, keepdims=True)
    i1 = jnp.min(jnp.where(lt == m1, sub, N_EXPERTS), axis=0, keepdims=True)
    rest = jnp.where(sub == i1, -jnp.inf, lt)
    m2 = jnp.max(rest, axis=0, keepdims=True)
    i2 = jnp.min(jnp.where(rest == m2, sub, N_EXPERTS), axis=0, keepdims=True)
    oh1 = sub == i1
    oh2 = sub == i2
    member = jnp.where(jnp.logical_or(oh1, oh2), 1.0, 0.0)
    prefix = jnp.dot(member.astype(BF16), triu_ref[...], preferred_element_type=F32)
    carry = carry_scr[...]
    base = carry[:, 0:1] + prefix
    rank1 = jnp.sum(jnp.where(oh1, base, 0.0), axis=0, keepdims=True)
    rank2 = jnp.sum(jnp.where(oh2, base, 0.0), axis=0, keepdims=True)
    carry_new = carry + jnp.sum(member, axis=1, keepdims=True)
    carry_scr[...] = carry_new
    cnt_ref[...] = carry_new.astype(I32)
    meta_ref[...] = jnp.where(
        sub == 0, i1, jnp.where(
            sub == 1, i2, jnp.where(
                sub == 2, rank1.astype(I32), jnp.where(
                    sub == 3, rank2.astype(I32), 0))))

    lane = lax.broadcasted_iota(I32, lr.shape, 1)
    lr = jnp.where(lane < N_EXPERTS, lr, -jnp.inf)
    m1r = jnp.max(lr, axis=-1, keepdims=True)
    i1r = jnp.min(jnp.where(lr == m1r, lane, LANES), axis=-1, keepdims=True)
    m2r = jnp.max(jnp.where(lane == i1r, -jnp.inf, lr), axis=-1, keepdims=True)
    e2 = jnp.exp(m2r - m1r)
    w1 = 1.0 / (1.0 + e2)
    w2 = e2 * w1
    wts_ref[...] = jnp.where(lane == 0, w1, jnp.where(lane == 1, w2, 0.0))


def _route(x, norm_w, router_w):
    t = x.shape[0]
    rwp = jnp.pad(router_w, ((0, 0), (0, LANES - N_EXPERTS)))
    triu = jnp.triu(jnp.ones((TM_ROUTE, TM_ROUTE), BF16), k=1)
    return pl.pallas_call(
        _route_kernel,
        out_shape=(
            jax.ShapeDtypeStruct((t, D_MODEL), F32),
            jax.ShapeDtypeStruct((SUBLANES, t), I32),
            jax.ShapeDtypeStruct((t, LANES), F32),
            jax.ShapeDtypeStruct((N_EXPERTS, LANES), I32),
        ),
        grid=(t // TM_ROUTE,),
        in_specs=[
            pl.BlockSpec((TM_ROUTE, D_MODEL), lambda i: (i, 0)),
            pl.BlockSpec((1, D_MODEL), lambda i: (0, 0)),
            pl.BlockSpec((D_MODEL, LANES), lambda i: (0, 0)),
            pl.BlockSpec((TM_ROUTE, TM_ROUTE), lambda i: (0, 0)),
        ],
        out_specs=(
            pl.BlockSpec((TM_ROUTE, D_MODEL), lambda i: (i, 0)),
            pl.BlockSpec((SUBLANES, TM_ROUTE), lambda i: (0, i)),
            pl.BlockSpec((TM_ROUTE, LANES), lambda i: (i, 0)),
            pl.BlockSpec((N_EXPERTS, LANES), lambda i: (0, 0)),
        ),
        scratch_shapes=[pltpu.VMEM((N_EXPERTS, LANES), F32)],
        compiler_params=pltpu.CompilerParams(
            dimension_semantics=("arbitrary",),
            vmem_limit_bytes=VMEM_LIMIT),
        name="moe_route",
    )(x, norm_w, rwp, triu)


def _store_rows_as_tiles(ref, value):
    n = value.shape[0]
    for c in range(ROW_TILE):
        ref[pl.ds(c, n, stride=ROW_TILE), :] = value[:, c * LANES:(c + 1) * LANES]


def _load_tile_chunk(ref, c, n):
    return ref[pl.ds(c, n, stride=ROW_TILE), :]


def _pos_copy(pos_hbm, pos_smem, sem, step, slot):
    n = pos_hbm.shape[1]
    dst = pos_smem.at[pl.ds(pl.multiple_of(slot * n, n), n)]
    return pltpu.make_async_copy(pos_hbm.at[step], dst, sem.at[slot])


def _dispatch_kernel(zt_ref, pos_hbm, h_ref, xs_hbm, pos_smem, hbuf, zbuf, psem, dsem, zsem):
    i = pl.program_id(0)
    n = pl.num_programs(0)
    slot = i % 2

    @pl.when(i == 0)
    def _():
        _pos_copy(pos_hbm, pos_smem, psem, 0, 0).start()

    _pos_copy(pos_hbm, pos_smem, psem, i, slot).wait()

    @pl.when(i + 1 < n)
    def _():
        _pos_copy(pos_hbm, pos_smem, psem, i + 1, 1 - slot).start()

    def wait_rows_from(s):
        for _ in range(TOP_K):
            pltpu.make_async_copy(hbuf.at[s], xs_hbm.at[pl.ds(0, TM_DISP * ROW_TILE)],
                                  dsem.at[s]).wait()

    @pl.when(i == 0)
    def _():
        zbuf[...] = jnp.zeros_like(zbuf)
        n_tiles = xs_hbm.shape[0] // (TM_GRP * ROW_TILE)

        def zero_tile_copy(k):
            r0 = pl.multiple_of(k * (TM_GRP * ROW_TILE), TM_GRP * ROW_TILE)
            return pltpu.make_async_copy(zbuf, xs_hbm.at[pl.ds(r0, TM_GRP * ROW_TILE)], zsem)

        def for_each_zero_tile(fn):
            for e in range(N_EXPERTS):
                @pl.when(zt_ref[N_EXPERTS + e] > 0)
                def _():
                    fn(zero_tile_copy(zt_ref[e]))

                @pl.when(zt_ref[2 * N_EXPERTS] + e < n_tiles)
                def _():
                    fn(zero_tile_copy(zt_ref[2 * N_EXPERTS] + e))

        for_each_zero_tile(lambda cp: cp.start())
        for_each_zero_tile(lambda cp: cp.wait())

    @pl.when(i >= 2)
    def _():
        wait_rows_from(slot)

    _store_rows_as_tiles(hbuf.at[slot], h_ref[...])

    pos_base = slot * (TOP_K * TM_DISP)

    def issue(t, carry):
        src = hbuf.at[slot, pl.ds(pl.multiple_of(t * ROW_TILE, ROW_TILE), ROW_TILE)]
        for j in range(TOP_K):
            p = pl.multiple_of(pos_smem[pos_base + j * TM_DISP + t], ROW_TILE)
            pltpu.make_async_copy(src, xs_hbm.at[pl.ds(p, ROW_TILE)],
                                  dsem.at[slot]).start(priority=j)
        return carry

    lax.fori_loop(0, TM_DISP, issue, 0, unroll=DMA_ISSUE_UNROLL)

    @pl.when(i == n - 1)
    def _():
        wait_rows_from(slot)

        @pl.when(n > 1)
        def _():
            wait_rows_from(1 - slot)


def _dispatch(h, pos_tiles, zero_tiles, n_sorted_rows):
    t = h.shape[0]
    return pl.pallas_call(
        _dispatch_kernel,
        out_shape=jax.ShapeDtypeStruct((n_sorted_rows * ROW_TILE, LANES), F32),
        grid_spec=pltpu.PrefetchScalarGridSpec(
            num_scalar_prefetch=1,
            grid=(t // TM_DISP,),
            in_specs=[pl.BlockSpec(memory_space=pl.ANY),
                      pl.BlockSpec((TM_DISP, D_MODEL), lambda i, zt: (i, 0))],
            out_specs=pl.BlockSpec(memory_space=pl.ANY),
            scratch_shapes=[
                pltpu.SMEM((2 * TOP_K * TM_DISP,), I32),
                pltpu.VMEM((2, TM_DISP * ROW_TILE, LANES), F32),
                pltpu.VMEM((TM_GRP * ROW_TILE, LANES), F32),
                pltpu.SemaphoreType.DMA((2,)),
                pltpu.SemaphoreType.DMA((2,)),
                pltpu.SemaphoreType.DMA(()),
            ]),
        compiler_params=pltpu.CompilerParams(
            dimension_semantics=("arbitrary",),
            vmem_limit_bytes=VMEM_LIMIT),
        name="moe_dispatch",
    )(zero_tiles, pos_tiles, h)


def _experts_kernel(te_ref, nv_ref, xs_ref, w1_ref, w3_ref, w2_ref, y_ref, xb_scr):
    k = pl.program_id(0)

    @pl.when(k < nv_ref[0])
    def _():
        for c in range(ROW_TILE):
            xb_scr[:, c * LANES:(c + 1) * LANES] = _load_tile_chunk(
                xs_ref, c, TM_GRP).astype(BF16)
        xb = xb_scr[...]
        y = None
        for lo, hi in zip(FF_SPLITS[:-1], FF_SPLITS[1:]):
            cols = slice(lo, hi)
            a = jnp.dot(xb, w1_ref[0, :, cols], preferred_element_type=F32)
            b = jnp.dot(xb, w3_ref[0, :, cols], preferred_element_type=F32)
            act = (_silu(a) * b).astype(BF16)
            part = jnp.dot(act, w2_ref[0, cols, :], preferred_element_type=F32)
            y = part if y is None else y + part
        _store_rows_as_tiles(y_ref, y)

    @pl.when(k >= nv_ref[0])
    def _():
        y_ref[...] = jnp.zeros_like(y_ref)


def _experts(xs, tile_expert, n_valid, w1, w3, w2):
    n_rows = xs.shape[0] // ROW_TILE

    def row_map(k, te, nv):
        return (jnp.minimum(k, nv[0] - 1), 0)

    def out_map(k, te, nv):
        return (k, 0)

    def w_map(k, te, nv):
        return (te[jnp.minimum(k, nv[0] - 1)], 0, 0)

    def weight_spec(shape):
        return pl.BlockSpec((1,) + shape, w_map, pipeline_mode=pl.Buffered(1))

    return pl.pallas_call(
        _experts_kernel,
        out_shape=jax.ShapeDtypeStruct((n_rows * ROW_TILE, LANES), F32),
        grid_spec=pltpu.PrefetchScalarGridSpec(
            num_scalar_prefetch=2,
            grid=(n_rows // TM_GRP,),
            in_specs=[
                pl.BlockSpec((TM_GRP * ROW_TILE, LANES), row_map),
                weight_spec((D_MODEL, D_FF)),
                weight_spec((D_MODEL, D_FF)),
                weight_spec((D_FF, D_MODEL)),
            ],
            out_specs=pl.BlockSpec((TM_GRP * ROW_TILE, LANES), out_map),
            scratch_shapes=[pltpu.VMEM((TM_GRP, D_MODEL), BF16)]),
        compiler_params=pltpu.CompilerParams(
            dimension_semantics=("arbitrary",),
            vmem_limit_bytes=VMEM_LIMIT),
        name="moe_experts",
    )(tile_expert, n_valid, xs, w1, w3, w2)


def _combine_kernel(fuse_norm, pos_hbm, y_hbm, x_ref, wts_ref, fnw_ref, o_ref,
                    pos_smem, ybuf, psem, gsem):
    i = pl.program_id(0)
    n = pl.num_programs(0)
    slot = i % 2

    def issue_gathers(s):
        pos_base = s * (TOP_K * TM_COMB)

        def body(t, carry):
            dst_rows = pl.ds(pl.multiple_of(t * ROW_TILE, ROW_TILE), ROW_TILE)
            for j in range(TOP_K):
                p = pl.multiple_of(pos_smem[pos_base + j * TM_COMB + t], ROW_TILE)
                pltpu.make_async_copy(y_hbm.at[pl.ds(p, ROW_TILE)],
                                      ybuf.at[s, j, dst_rows], gsem.at[s]).start(priority=j)
            return carry

        lax.fori_loop(0, TM_COMB, body, 0, unroll=DMA_ISSUE_UNROLL)

    @pl.when(i == 0)
    def _():
        first = _pos_copy(pos_hbm, pos_smem, psem, 0, 0)
        first.start()
        first.wait()
        issue_gathers(0)

        @pl.when(n > 1)
        def _():
            _pos_copy(pos_hbm, pos_smem, psem, 1, 1).start()

    @pl.when(i + 1 < n)
    def _():
        _pos_copy(pos_hbm, pos_smem, psem, i + 1, 1 - slot).wait()
        issue_gathers(1 - slot)

        @pl.when(i + 2 < n)
        def _():
            _pos_copy(pos_hbm, pos_smem, psem, i + 2, slot).start()

    for j in range(TOP_K):
        pltpu.make_async_copy(y_hbm.at[pl.ds(0, TM_COMB * ROW_TILE)], ybuf.at[slot, j],
                              gsem.at[slot]).wait()

    w = wts_ref[...]
    w1 = w[:, 0:1]
    w2 = w[:, 1:2]
    chunks = []
    for c in range(ROW_TILE):
        chunks.append(x_ref[:, c * LANES:(c + 1) * LANES]
                      + w1 * _load_tile_chunk(ybuf.at[slot, 0], c, TM_COMB)
                      + w2 * _load_tile_chunk(ybuf.at[slot, 1], c, TM_COMB))
    if fuse_norm:
        ss = chunks[0] * chunks[0]
        for ch in chunks[1:]:
            ss = ss + ch * ch
        inv = lax.rsqrt(jnp.sum(ss, axis=-1, keepdims=True) * (1.0 / D_MODEL) + EPS)
        fnw = fnw_ref[...]
        chunks = [ch * inv * fnw[:, c * LANES:(c + 1) * LANES] for c, ch in enumerate(chunks)]
    for c, ch in enumerate(chunks):
        o_ref[:, c * LANES:(c + 1) * LANES] = ch


def _combine(x, y, pos_tiles, wts, final_norm_w, fuse_norm):
    t = x.shape[0]
    return pl.pallas_call(
        functools.partial(_combine_kernel, fuse_norm),
        out_shape=jax.ShapeDtypeStruct((t, D_MODEL), F32),
        grid=(t // TM_COMB,),
        in_specs=[
            pl.BlockSpec(memory_space=pl.ANY),
            pl.BlockSpec(memory_space=pl.ANY),
            pl.BlockSpec((TM_COMB, D_MODEL), lambda i: (i, 0)),
            pl.BlockSpec((TM_COMB, LANES), lambda i: (i, 0)),
            pl.BlockSpec((1, D_MODEL), lambda i: (0, 0)),
        ],
        out_specs=pl.BlockSpec((TM_COMB, D_MODEL), lambda i: (i, 0)),
        scratch_shapes=[
            pltpu.SMEM((2 * TOP_K * TM_COMB,), I32),
            pltpu.VMEM((2, TOP_K, TM_COMB * ROW_TILE, LANES), F32),
            pltpu.SemaphoreType.DMA((2,)),
            pltpu.SemaphoreType.DMA((2,)),
        ],
        compiler_params=pltpu.CompilerParams(dimension_semantics=("arbitrary",)),
        name="moe_combine",
    )(pos_tiles, y, x, wts, final_norm_w)


def _tile_major(pos, tile):
    t = pos.shape[1]
    return pos.reshape(TOP_K, t // tile, tile).transpose(1, 0, 2).reshape(t // tile, TOP_K * tile)


def _moe_ffn(x, norm_w, router_w, w1, w3, w2, final_norm_w, fuse_norm):
    t = x.shape[0]
    h, meta, wts, cnt = _route(x, norm_w, router_w)

    counts = cnt[:, 0]
    padded = ((counts + TM_GRP - 1) // TM_GRP) * TM_GRP
    ends = jnp.cumsum(padded)
    offs = ends - padded
    n_sorted_rows = TOP_K * t + N_EXPERTS * TM_GRP
    tile_start = jnp.arange(n_sorted_rows // TM_GRP, dtype=I32) * TM_GRP
    tile_expert = jnp.minimum(
        jnp.sum((tile_start[:, None] >= ends[None, :]).astype(I32), axis=1), N_EXPERTS - 1)
    n_valid = (ends[-1] // TM_GRP).astype(I32).reshape(1)
    pos = (jnp.stack([offs[meta[0]] + meta[2], offs[meta[1]] + meta[3]]) * ROW_TILE).astype(I32)
    zero_tiles = jnp.concatenate([ends // TM_GRP - 1, padded, n_valid]).astype(I32)

    xs = _dispatch(h, _tile_major(pos, TM_DISP), zero_tiles, n_sorted_rows)
    y = _experts(xs, tile_expert.astype(I32), n_valid, w1, w3, w2)
    return _combine(x, y, _tile_major(pos, TM_COMB), wts, final_norm_w, fuse_norm)


def _final_norm_kernel(x_ref, nw_ref, o_ref):
    o_ref[...] = _rms_norm_rows(x_ref[...], nw_ref[...])


def _final_norm(x, norm_w):
    t = x.shape[0]
    return pl.pallas_call(
        _final_norm_kernel,
        out_shape=jax.ShapeDtypeStruct((t, D_MODEL), F32),
        grid=(t // TM_PROJ,),
        in_specs=[
            pl.BlockSpec((TM_PROJ, D_MODEL), lambda i: (i, 0)),
            pl.BlockSpec((1, D_MODEL), lambda i: (0, 0)),
        ],
        out_specs=pl.BlockSpec((TM_PROJ, D_MODEL), lambda i: (i, 0)),
        compiler_params=pltpu.CompilerParams(dimension_semantics=("arbitrary",)),
        name="final_norm",
    )(x, norm_w)


def kernel(x, w_in, lower_bounds, hgrn_norm_w, conv_w, w_proj_hgrn, w_proj_conv, w_out,
           norm_mix, norm_ffn, dense_w1, dense_w3, dense_w2, router_w,
           expert_w1, expert_w3, expert_w2, final_norm):
    batch, seq, d = x.shape
    depth = w_in.shape[0]
    assert d == D_MODEL and seq % CT_REC == 0 and (batch * seq) % TM_PROJ == 0
    assert w_in.shape[2] == N_GROUPS * D_MODEL

    xt = x.reshape(batch * seq, d)
    tri = jnp.tril(jnp.ones((CHUNK, CHUNK), F32))
    lower_bounds = lower_bounds.astype(F32)
    final_w = final_norm.reshape(1, d)
    normed = False

    for layer in range(depth):
        proj = _inproj(xt, norm_mix[layer].reshape(1, d), w_in[layer].astype(BF16))
        o_gated = _recurrence(proj, lower_bounds, hgrn_norm_w[layer].reshape(1, HEAD_DIM),
                              tri, layer, batch, seq)
        xt = _mixout(xt, o_gated, proj, conv_w[layer],
                     w_proj_hgrn[layer].astype(BF16), w_proj_conv[layer].astype(BF16),
                     w_out[layer].astype(BF16), seq)
        j = layer // 2
        nw = norm_ffn[layer].reshape(1, d)
        if layer % 2 == 0:
            xt = _dense_ffn(xt, nw, dense_w1[j].astype(BF16), dense_w3[j].astype(BF16),
                            dense_w2[j].astype(BF16))
        else:
            normed = layer == depth - 1
            xt = _moe_ffn(xt, nw, router_w[j].astype(F32), expert_w1[j].astype(BF16),
                          expert_w3[j].astype(BF16), expert_w2[j].astype(BF16),
                          final_w, normed)
    out = xt if normed else _final_norm(xt, final_w)
    return out.reshape(batch, seq, d)
```

```python
import functools

import jax
import jax.numpy as jnp
from jax import lax
from jax.experimental import pallas as pl
from jax.experimental.pallas import tpu as pltpu

D_MODEL = 1024
HEADS = 8
HEAD_DIM = 128
HGRN_SCALE = HEAD_DIM ** -0.5
F_MIN = 1e-6
CONV_K = 3
D_FF = 2816
N_EXPERTS = 8
TOP_K = 2
EPS = 1e-6
LOG2_E = 1.4426950408889634
N_GROUPS = 9
G_Q, G_F, G_I, G_G, G_B, G_C, G_U, G_GA, G_GB = range(N_GROUPS)

LANES = 128
SUBLANES = 8
ROW_TILE = SUBLANES
BF16_SUBLANES = 16
VMEM_LIMIT = 56 * 1024 * 1024

TM_PROJ = 512
TN_PROJ = 2304
TM_MIX = 512
TM_FFN = 512
MXU_TILE = 256
FF_SPLITS = (0, 6 * MXU_TILE, D_FF)
TM_ROUTE = 512
TM_DISP = 512
TM_GRP = 512
TM_COMB = 512
DMA_ISSUE_UNROLL = 8
CT_REC = 1024
CHUNK = 128
SUB = 32
N_SUB = CHUNK // SUB
MAX_EXP2 = 126.0
REC_UNROLL = 4

F32 = jnp.float32
BF16 = jnp.bfloat16
I32 = jnp.int32


def _rms_norm_rows(x, w):
    ms = jnp.mean(x * x, axis=-1, keepdims=True)
    return x * lax.rsqrt(ms + EPS) * w


def _exp_neg(x):
    return jnp.exp2(x * (-LOG2_E))


def _sigmoid(x):
    return 1.0 / (1.0 + _exp_neg(x))


def _silu(x):
    return x * _sigmoid(x)


def _split_bf16(v):
    hi = v.astype(BF16)
    lo = (v - hi.astype(F32)).astype(BF16)
    return hi, lo


def _dot3(a_hi, a_lo, b_hi, b_lo, dims):
    dg = functools.partial(lax.dot_general, dimension_numbers=dims,
                           preferred_element_type=F32)
    return dg(a_hi, b_hi) + dg(a_hi, b_lo) + dg(a_lo, b_hi)


def _inproj_kernel(x_ref, nw_ref, w_ref, o_ref):
    h = _rms_norm_rows(x_ref[...], nw_ref[...]).astype(BF16)
    for c in range(w_ref.shape[1] // TN_PROJ):
        cols = slice(c * TN_PROJ, (c + 1) * TN_PROJ)
        o_ref[:, cols] = jnp.dot(h, w_ref[:, cols],
                                 preferred_element_type=F32).astype(o_ref.dtype)


def _inproj(x, norm_w, w_in_bf16):
    t = x.shape[0]
    n = w_in_bf16.shape[1]
    assert n % TN_PROJ == 0
    return pl.pallas_call(
        _inproj_kernel,
        out_shape=jax.ShapeDtypeStruct((t, n), BF16),
        grid=(t // TM_PROJ,),
        in_specs=[
            pl.BlockSpec((TM_PROJ, D_MODEL), lambda i: (i, 0)),
            pl.BlockSpec((1, D_MODEL), lambda i: (0, 0)),
            pl.BlockSpec((D_MODEL, n), lambda i: (0, 0), pipeline_mode=pl.Buffered(1)),
        ],
        out_specs=pl.BlockSpec((TM_PROJ, n), lambda i: (i, 0)),
        compiler_params=pltpu.CompilerParams(
            dimension_semantics=("arbitrary",),
            vmem_limit_bytes=VMEM_LIMIT),
        name="inproj",
    )(x, norm_w, w_in_bf16)


def _layer_lower_bound(lb_raw, layer):
    m = jnp.max(lb_raw, axis=0, keepdims=True)
    e = jnp.exp(lb_raw - m)
    soft = e / jnp.sum(e, axis=0, keepdims=True)
    acc = soft[0:1, :]
    for j in range(1, layer + 1):
        acc = acc + soft[j:j + 1, :]
    return acc - soft[0:1, :]


def _rec_chunk(rows, slot, q_ref, f_ref, i_ref, g_ref, o_ref, st_ref, lb, gw, tri_f32, tri_bf,
               lhs_s, rhs_s, kst_s, sg_s):
    f = lb + (1.0 - lb) * _sigmoid(f_ref[rows, :].astype(F32))
    logf = jnp.log2(jnp.maximum(f, F_MIN))
    kk = 1.0 - f

    hi, lo = _split_bf16(logf)
    cum2 = jnp.dot(tri_bf, jnp.concatenate([hi, lo], axis=1), preferred_element_type=F32)
    cum = cum2[:, :D_MODEL] + cum2[:, D_MODEL:]

    s_ref = [jnp.zeros((1, D_MODEL), F32)]
    for j in range(1, N_SUB):
        s_ref.append(cum[j * SUB - 1:j * SUB, :])
    cum_end = cum[CHUNK - 1:CHUNK, :]
    decay_end = jnp.exp2(cum_end)

    qh = _silu(q_ref[rows, :].astype(F32))
    for j in range(N_SUB):
        blk = slice(j * SUB, (j + 1) * SUB)
        q_loc = qh[blk] * jnp.exp2(cum[blk] - s_ref[j])
        k_loc = kk[blk] * jnp.exp2(jnp.minimum(s_ref[j] - cum[blk], MAX_EXP2))
        rhs_s[slot, blk, j * D_MODEL:(j + 1) * D_MODEL] = k_loc.astype(BF16)
        kst_s[slot, blk, :] = (k_loc * jnp.exp2(cum_end - s_ref[j])).astype(BF16)
        for jj in range(j + 1):
            val = q_loc if jj == j else q_loc * jnp.exp2(s_ref[j] - s_ref[jj])
            lhs_s[slot, blk, jj * D_MODEL:(jj + 1) * D_MODEL] = val.astype(BF16)
    sg_s[slot] = _silu(g_ref[rows, :].astype(F32))

    for h in range(HEADS):
        cols = slice(h * HEAD_DIM, (h + 1) * HEAD_DIM)
        seg_cols = [slice(j * D_MODEL + h * HEAD_DIM, j * D_MODEL + (h + 1) * HEAD_DIM)
                    for j in range(N_SUB)]
        lhs = jnp.concatenate([lhs_s[slot, :, sc] for sc in seg_cols], axis=1)
        rhs = jnp.concatenate([rhs_s[slot, :, sc] for sc in seg_cols], axis=1)
        scores = lax.dot_general(lhs, rhs, (((1,), (1,)), ((), ())),
                                 preferred_element_type=F32) * tri_f32
        v_bf = i_ref[rows, cols]
        st = st_ref[h]
        o = (jnp.dot(scores.astype(BF16), v_bf, preferred_element_type=F32)
             + lax.dot_general(lhs_s[slot, :, seg_cols[0]], st.astype(BF16),
                               (((1,), (1,)), ((), ())), preferred_element_type=F32))
        st_ref[h] = st * decay_end[:, cols] + lax.dot_general(
            v_bf, kst_s[slot, :, cols], (((0,), (0,)), ((), ())), preferred_element_type=F32)
        ms = jnp.mean(o * o, axis=-1, keepdims=True)
        out = o * lax.rsqrt(ms + EPS / (HGRN_SCALE * HGRN_SCALE)) * gw * sg_s[slot, :, cols]
        o_ref[rows, cols] = out.astype(o_ref.dtype)


def _rec_kernel(layer, q_ref, f_ref, i_ref, g_ref, lbraw_ref, gw_ref, tri_ref,
                 o_ref, st_ref, lhs_s, rhs_s, kst_s, sg_s):
    @pl.when(pl.program_id(1) == 0)
    def _():
        st_ref[...] = jnp.zeros_like(st_ref)
        lhs_s[...] = jnp.zeros_like(lhs_s)
        rhs_s[...] = jnp.zeros_like(rhs_s)

    lb = _layer_lower_bound(lbraw_ref[...], layer)
    gw = gw_ref[...]
    tri_f32 = tri_ref[...]
    tri_bf = tri_f32.astype(BF16)

    def body(cp, carry):
        for u in range(REC_UNROLL):
            r0 = pl.multiple_of((cp * REC_UNROLL + u) * CHUNK, CHUNK)
            _rec_chunk(pl.ds(r0, CHUNK), u, q_ref, f_ref, i_ref, g_ref, o_ref, st_ref,
                       lb, gw, tri_f32, tri_bf, lhs_s, rhs_s, kst_s, sg_s)
        return carry

    lax.fori_loop(0, CT_REC // (CHUNK * REC_UNROLL), body, 0)


def _recurrence(proj, lower_bounds, gnorm_w, tri, layer, batch, seq):
    t = proj.shape[0]
    steps = seq // CT_REC

    def group_spec(g):
        return pl.BlockSpec((CT_REC, D_MODEL), lambda b, s: (b * steps + s, g))

    return pl.pallas_call(
        functools.partial(_rec_kernel, layer),
        out_shape=jax.ShapeDtypeStruct((t, D_MODEL), BF16),
        grid=(batch, steps),
        in_specs=[
            group_spec(G_Q), group_spec(G_F), group_spec(G_I), group_spec(G_G),
            pl.BlockSpec(lower_bounds.shape, lambda b, s: (0, 0)),
            pl.BlockSpec((1, HEAD_DIM), lambda b, s: (0, 0)),
            pl.BlockSpec((CHUNK, CHUNK), lambda b, s: (0, 0)),
        ],
        out_specs=pl.BlockSpec((CT_REC, D_MODEL), lambda b, s: (b * steps + s, 0)),
        scratch_shapes=[
            pltpu.VMEM((HEADS, HEAD_DIM, HEAD_DIM), F32),
            pltpu.VMEM((REC_UNROLL, CHUNK, N_SUB * D_MODEL), BF16),
            pltpu.VMEM((REC_UNROLL, CHUNK, N_SUB * D_MODEL), BF16),
            pltpu.VMEM((REC_UNROLL, CHUNK, D_MODEL), BF16),
            pltpu.VMEM((REC_UNROLL, CHUNK, D_MODEL), F32),
        ],
        compiler_params=pltpu.CompilerParams(
            dimension_semantics=("arbitrary", "arbitrary"),
            vmem_limit_bytes=VMEM_LIMIT),
        name="hgrn2_recurrence",
    )(proj, proj, proj, proj, lower_bounds, gnorm_w, tri)


def _mixout_kernel(seq, x_ref, oa_ref, b_ref, c_ref, u_ref, ga_ref, gb_ref,
                   bh_ref, uh_ref, cw_ref, wph_ref, wpc_ref, wo_ref, o_ref):
    i = pl.program_id(0)
    v = b_ref[...].astype(F32) * u_ref[...].astype(F32)
    at_seq_start = (i * TM_MIX) % seq == 0
    halo = bh_ref[...].astype(F32) * uh_ref[...].astype(F32)
    halo = jnp.where(at_seq_start, 0.0, halo)
    prev1 = halo[BF16_SUBLANES - 1:BF16_SUBLANES, :]
    prev2 = halo[BF16_SUBLANES - 2:BF16_SUBLANES - 1, :]
    row = lax.broadcasted_iota(I32, v.shape, 0)
    v1 = jnp.where(row == 0, prev1, pltpu.roll(v, 1, 0))
    v2 = jnp.where(row == 0, prev2,
                   jnp.where(row == 1, prev1, pltpu.roll(v, 2, 0)))
    cw = cw_ref[...]
    conv = cw[0:1, :] * v2 + cw[1:2, :] * v1 + cw[2:3, :] * v
    yb_in = (c_ref[...].astype(F32) * conv).astype(BF16)
    y_b = jnp.dot(yb_in, wpc_ref[...], preferred_element_type=F32)
    y_a = jnp.dot(oa_ref[...], wph_ref[...], preferred_element_type=F32)
    merged = (_sigmoid(ga_ref[...].astype(F32)) * y_a
              + _sigmoid(gb_ref[...].astype(F32)) * y_b).astype(BF16)
    o_ref[...] = x_ref[...] + jnp.dot(merged, wo_ref[...], preferred_element_type=F32)


def _mixout(x, o_gated, proj, conv_w, wph, wpc, wo, seq):
    t = x.shape[0]
    halo_per_tile = TM_MIX // BF16_SUBLANES

    def group_spec(g):
        return pl.BlockSpec((TM_MIX, D_MODEL), lambda i: (i, g))

    def halo_spec(g):
        return pl.BlockSpec(
            (BF16_SUBLANES, D_MODEL),
            lambda i: (jnp.maximum(i * halo_per_tile - 1, 0), g))

    def weight_spec():
        return pl.BlockSpec((D_MODEL, D_MODEL), lambda i: (0, 0))

    return pl.pallas_call(
        functools.partial(_mixout_kernel, seq),
        out_shape=jax.ShapeDtypeStruct((t, D_MODEL), F32),
        grid=(t // TM_MIX,),
        in_specs=[
            pl.BlockSpec((TM_MIX, D_MODEL), lambda i: (i, 0)),
            pl.BlockSpec((TM_MIX, D_MODEL), lambda i: (i, 0)),
            group_spec(G_B), group_spec(G_C), group_spec(G_U),
            group_spec(G_GA), group_spec(G_GB),
            halo_spec(G_B), halo_spec(G_U),
            pl.BlockSpec((CONV_K, D_MODEL), lambda i: (0, 0)),
            weight_spec(), weight_spec(), weight_spec(),
        ],
        out_specs=pl.BlockSpec((TM_MIX, D_MODEL), lambda i: (i, 0)),
        compiler_params=pltpu.CompilerParams(
            dimension_semantics=("arbitrary",),
            vmem_limit_bytes=VMEM_LIMIT),
        name="mixer_out",
    )(x, o_gated, proj, proj, proj, proj, proj, proj, proj, conv_w, wph, wpc, wo)


def _dense_ffn_kernel(x_ref, nw_ref, w1_ref, w3_ref, w2_ref, o_ref):
    x = x_ref[...]
    h = _rms_norm_rows(x, nw_ref[...]).astype(BF16)
    a = jnp.dot(h, w1_ref[...], preferred_element_type=F32)
    b = jnp.dot(h, w3_ref[...], preferred_element_type=F32)
    act = (_silu(a) * b).astype(BF16)
    o_ref[...] = x + jnp.dot(act, w2_ref[...], preferred_element_type=F32)


def _dense_ffn(x, norm_w, w1, w3, w2):
    t = x.shape[0]
    return pl.pallas_call(
        _dense_ffn_kernel,
        out_shape=jax.ShapeDtypeStruct((t, D_MODEL), F32),
        grid=(t // TM_FFN,),
        in_specs=[
            pl.BlockSpec((TM_FFN, D_MODEL), lambda i: (i, 0)),
            pl.BlockSpec((1, D_MODEL), lambda i: (0, 0)),
            pl.BlockSpec((D_MODEL, D_FF), lambda i: (0, 0)),
            pl.BlockSpec((D_MODEL, D_FF), lambda i: (0, 0)),
            pl.BlockSpec((D_FF, D_MODEL), lambda i: (0, 0)),
        ],
        out_specs=pl.BlockSpec((TM_FFN, D_MODEL), lambda i: (i, 0)),
        compiler_params=pltpu.CompilerParams(
            dimension_semantics=("arbitrary",),
            vmem_limit_bytes=VMEM_LIMIT),
        name="dense_ffn",
    )(x, norm_w, w1, w3, w2)


def _route_kernel(x_ref, nw_ref, rwp_ref, triu_ref,
                  h_ref, meta_ref, wts_ref, cnt_ref, carry_scr):
    @pl.when(pl.program_id(0) == 0)
    def _():
        carry_scr[...] = jnp.zeros_like(carry_scr)

    h = _rms_norm_rows(x_ref[...], nw_ref[...])
    h_ref[...] = h
    h_hi, h_lo = _split_bf16(h)
    rwp_hi, rwp_lo = _split_bf16(rwp_ref[...])
    lr = _dot3(h_hi, h_lo, rwp_hi, rwp_lo, (((1,), (0,)), ((), ())))

    lt = jnp.transpose(lr)[:N_EXPERTS, :]
    sub = lax.broadcasted_iota(I32, lt.shape, 0)
    m1 = jnp.max(lt, axis=0, keepdims=True)
    i1 = jnp.min(jnp.where(lt == m1, sub, N_EXPERTS), axis=0, keepdims=True)
    rest = jnp.where(sub == i1, -jnp.inf, lt)
    m2 = jnp.max(rest, axis=0, keepdims=True)
    i2 = jnp.min(jnp.where(rest == m2, sub, N_EXPERTS), axis=0, keepdims=True)
    oh1 = sub == i1
    oh2 = sub == i2
    member = jnp.where(jnp.logical_or(oh1, oh2), 1.0, 0.0)
    prefix = jnp.dot(member.astype(BF16), triu_ref[...], preferred_element_type=F32)
    carry = carry_scr[...]
    base = carry[:, 0:1] + prefix
    rank1 = jnp.sum(jnp.where(oh1, base, 0.0), axis=0, keepdims=True)
    rank2 = jnp.sum(jnp.where(oh2, base, 0.0), axis=0, keepdims=True)
    carry_new = carry + jnp.sum(member, axis=1, keepdims=True)
    carry_scr[...] = carry_new
    cnt_ref[...] = carry_new.astype(I32)
    meta_ref[...] = jnp.where(
        sub == 0, i1, jnp.where(
            sub == 1, i2, jnp.where(
                sub == 2, rank1.astype(I32), jnp.where(
                    sub == 3, rank2.astype(I32), 0))))

    lane = lax.broadcasted_iota(I32, lr.shape, 1)
    lr = jnp.where(lane < N_EXPERTS, lr, -jnp.inf)
    m1r = jnp.max(lr, axis=-1, keepdims=True)
    i1r = jnp.min(jnp.where(lr == m1r, lane, LANES), axis=-1, keepdims=True)
    m2r = jnp.max(jnp.where(lane == i1r, -jnp.inf, lr), axis=-1, keepdims=True)
    e2 = jnp.exp(m2r - m1r)
    w1 = 1.0 / (1.0 + e2)
    w2 = e2 * w1
    wts_ref[...] = jnp.where(lane == 0, w1, jnp.where(lane == 1, w2, 0.0))


def _route(x, norm_w, router_w):
    t = x.shape[0]
    rwp = jnp.pad(router_w, ((0, 0), (0, LANES - N_EXPERTS)))
    triu = jnp.triu(jnp.ones((TM_ROUTE, TM_ROUTE), BF16), k=1)
    return pl.pallas_call(
        _route_kernel,
        out_shape=(
            jax.ShapeDtypeStruct((t, D_MODEL), F32),
            jax.ShapeDtypeStruct((SUBLANES, t), I32),
            jax.ShapeDtypeStruct((t, LANES), F32),
            jax.ShapeDtypeStruct((N_EXPERTS, LANES), I32),
        ),
        grid=(t // TM_ROUTE,),
        in_specs=[
            pl.BlockSpec((TM_ROUTE, D_MODEL), lambda i: (i, 0)),
            pl.BlockSpec((1, D_MODEL), lambda i: (0, 0)),
            pl.BlockSpec((D_MODEL, LANES), lambda i: (0, 0)),
            pl.BlockSpec((TM_ROUTE, TM_ROUTE), lambda i: (0, 0)),
        ],
        out_specs=(
            pl.BlockSpec((TM_ROUTE, D_MODEL), lambda i: (i, 0)),
            pl.BlockSpec((SUBLANES, TM_ROUTE), lambda i: (0, i)),
            pl.BlockSpec((TM_ROUTE, LANES), lambda i: (i, 0)),
            pl.BlockSpec((N_EXPERTS, LANES), lambda i: (0, 0)),
        ),
        scratch_shapes=[pltpu.VMEM((N_EXPERTS, LANES), F32)],
        compiler_params=pltpu.CompilerParams(
            dimension_semantics=("arbitrary",),
            vmem_limit_bytes=VMEM_LIMIT),
        name="moe_route",
    )(x, norm_w, rwp, triu)


def _store_rows_as_tiles(ref, value):
    n = value.shape[0]
    for c in range(ROW_TILE):
        ref[pl.ds(c, n, stride=ROW_TILE), :] = value[:, c * LANES:(c + 1) * LANES]


def _load_tile_chunk(ref, c, n):
    return ref[pl.ds(c, n, stride=ROW_TILE), :]


def _pos_copy(pos_hbm, pos_smem, sem, step, slot):
    n = pos_hbm.shape[1]
    dst = pos_smem.at[pl.ds(pl.multiple_of(slot * n, n), n)]
    return pltpu.make_async_copy(pos_hbm.at[step], dst, sem.at[slot])


def _dispatch_kernel(zt_ref, pos_hbm, h_ref, xs_hbm, pos_smem, hbuf, zbuf, psem, dsem, zsem):
    i = pl.program_id(0)
    n = pl.num_programs(0)
    slot = i % 2

    @pl.when(i == 0)
    def _():
        _pos_copy(pos_hbm, pos_smem, psem, 0, 0).start()

    _pos_copy(pos_hbm, pos_smem, psem, i, slot).wait()

    @pl.when(i + 1 < n)
    def _():
        _pos_copy(pos_hbm, pos_smem, psem, i + 1, 1 - slot).start()

    def wait_rows_from(s):
        for _ in range(TOP_K):
            pltpu.make_async_copy(hbuf.at[s], xs_hbm.at[pl.ds(0, TM_DISP * ROW_TILE)],
                                  dsem.at[s]).wait()

    @pl.when(i == 0)
    def _():
        zbuf[...] = jnp.zeros_like(zbuf)
        n_tiles = xs_hbm.shape[0] // (TM_GRP * ROW_TILE)

        def zero_tile_copy(k):
            r0 = pl.multiple_of(k * (TM_GRP * ROW_TILE), TM_GRP * ROW_TILE)
            return pltpu.make_async_copy(zbuf, xs_hbm.at[pl.ds(r0, TM_GRP * ROW_TILE)], zsem)

        def for_each_zero_tile(fn):
            for e in range(N_EXPERTS):
                @pl.when(zt_ref[N_EXPERTS + e] > 0)
                def _():
                    fn(zero_tile_copy(zt_ref[e]))

                @pl.when(zt_ref[2 * N_EXPERTS] + e < n_tiles)
                def _():
                    fn(zero_tile_copy(zt_ref[2 * N_EXPERTS] + e))

        for_each_zero_tile(lambda cp: cp.start())
        for_each_zero_tile(lambda cp: cp.wait())

    @pl.when(i >= 2)
    def _():
        wait_rows_from(slot)

    _store_rows_as_tiles(hbuf.at[slot], h_ref[...])

    pos_base = slot * (TOP_K * TM_DISP)

    def issue(t, carry):
        src = hbuf.at[slot, pl.ds(pl.multiple_of(t * ROW_TILE, ROW_TILE), ROW_TILE)]
        for j in range(TOP_K):
            p = pl.multiple_of(pos_smem[pos_base + j * TM_DISP + t], ROW_TILE)
            pltpu.make_async_copy(src, xs_hbm.at[pl.ds(p, ROW_TILE)],
                                  dsem.at[slot]).start(priority=j)
        return carry

    lax.fori_loop(0, TM_DISP, issue, 0, unroll=DMA_ISSUE_UNROLL)

    @pl.when(i == n - 1)
    def _():
        wait_rows_from(slot)

        @pl.when(n > 1)
        def _():
            wait_rows_from(1 - slot)


def _dispatch(h, pos_tiles, zero_tiles, n_sorted_rows):
    t = h.shape[0]
    return pl.pallas_call(
        _dispatch_kernel,
        out_shape=jax.ShapeDtypeStruct((n_sorted_rows * ROW_TILE, LANES), F32),
        grid_spec=pltpu.PrefetchScalarGridSpec(
            num_scalar_prefetch=1,
            grid=(t // TM_DISP,),
            in_specs=[pl.BlockSpec(memory_space=pl.ANY),
                      pl.BlockSpec((TM_DISP, D_MODEL), lambda i, zt: (i, 0))],
            out_specs=pl.BlockSpec(memory_space=pl.ANY),
            scratch_shapes=[
                pltpu.SMEM((2 * TOP_K * TM_DISP,), I32),
                pltpu.VMEM((2, TM_DISP * ROW_TILE, LANES), F32),
                pltpu.VMEM((TM_GRP * ROW_TILE, LANES), F32),
                pltpu.SemaphoreType.DMA((2,)),
                pltpu.SemaphoreType.DMA((2,)),
                pltpu.SemaphoreType.DMA(()),
            ]),
        compiler_params=pltpu.CompilerParams(
            dimension_semantics=("arbitrary",),
            vmem_limit_bytes=VMEM_LIMIT),
        name="moe_dispatch",
    )(zero_tiles, pos_tiles, h)


def _experts_kernel(te_ref, nv_ref, xs_ref, w1_ref, w3_ref, w2_ref, y_ref, xb_scr):
    k = pl.program_id(0)

    @pl.when(k < nv_ref[0])
    def _():
        for c in range(ROW_TILE):
            xb_scr[:, c * LANES:(c + 1) * LANES] = _load_tile_chunk(
                xs_ref, c, TM_GRP).astype(BF16)
        xb = xb_scr[...]
        y = None
        for lo, hi in zip(FF_SPLITS[:-1], FF_SPLITS[1:]):
            cols = slice(lo, hi)
            a = jnp.dot(xb, w1_ref[0, :, cols], preferred_element_type=F32)
            b = jnp.dot(xb, w3_ref[0, :, cols], preferred_element_type=F32)
            act = (_silu(a) * b).astype(BF16)
            part = jnp.dot(act, w2_ref[0, cols, :], preferred_element_type=F32)
            y = part if y is None else y + part
        _store_rows_as_tiles(y_ref, y)

    @pl.when(k >= nv_ref[0])
    def _():
        y_ref[...] = jnp.zeros_like(y_ref)


def _experts(xs, tile_expert, n_valid, w1, w3, w2):
    n_rows = xs.shape[0] // ROW_TILE

    def row_map(k, te, nv):
        return (jnp.minimum(k, nv[0] - 1), 0)

    def out_map(k, te, nv):
        return (k, 0)

    def w_map(k, te, nv):
        return (te[jnp.minimum(k, nv[0] - 1)], 0, 0)

    def weight_spec(shape):
        return pl.BlockSpec((1,) + shape, w_map, pipeline_mode=pl.Buffered(1))

    return pl.pallas_call(
        _experts_kernel,
        out_shape=jax.ShapeDtypeStruct((n_rows * ROW_TILE, LANES), F32),
        grid_spec=pltpu.PrefetchScalarGridSpec(
            num_scalar_prefetch=2,
            grid=(n_rows // TM_GRP,),
            in_specs=[
                pl.BlockSpec((TM_GRP * ROW_TILE, LANES), row_map),
                weight_spec((D_MODEL, D_FF)),
                weight_spec((D_MODEL, D_FF)),
                weight_spec((D_FF, D_MODEL)),
            ],
            out_specs=pl.BlockSpec((TM_GRP * ROW_TILE, LANES), out_map),
            scratch_shapes=[pltpu.VMEM((TM_GRP, D_MODEL), BF16)]),
        compiler_params=pltpu.CompilerParams(
            dimension_semantics=("arbitrary",),
            vmem_limit_bytes=VMEM_LIMIT),
        name="moe_experts",
    )(tile_expert, n_valid, xs, w1, w3, w2)


def _combine_kernel(fuse_norm, pos_hbm, y_hbm, x_ref, wts_ref, fnw_ref, o_ref,
                    pos_smem, ybuf, psem, gsem):
    i = pl.program_id(0)
    n = pl.num_programs(0)
    slot = i % 2

    def issue_gathers(s):
        pos_base = s * (TOP_K * TM_COMB)

        def body(t, carry):
            dst_rows = pl.ds(pl.multiple_of(t * ROW_TILE, ROW_TILE), ROW_TILE)
            for j in range(TOP_K):
                p = pl.multiple_of(pos_smem[pos_base + j * TM_COMB + t], ROW_TILE)
                pltpu.make_async_copy(y_hbm.at[pl.ds(p, ROW_TILE)],
                                      ybuf.at[s, j, dst_rows], gsem.at[s]).start(priority=j)
            return carry

        lax.fori_loop(0, TM_COMB, body, 0, unroll=DMA_ISSUE_UNROLL)

    @pl.when(i == 0)
    def _():
        first = _pos_copy(pos_hbm, pos_smem, psem, 0, 0)
        first.start()
        first.wait()
        issue_gathers(0)

        @pl.when(n > 1)
        def _():
            _pos_copy(pos_hbm, pos_smem, psem, 1, 1).start()

    @pl.when(i + 1 < n)
    def _():
        _pos_copy(pos_hbm, pos_smem, psem, i + 1, 1 - slot).wait()
        issue_gathers(1 - slot)

        @pl.when(i + 2 < n)
        def _():
            _pos_copy(pos_hbm, pos_smem, psem, i + 2, slot).start()

    for j in range(TOP_K):
        pltpu.make_async_copy(y_hbm.at[pl.ds(0, TM_COMB * ROW_TILE)], ybuf.at[slot, j],
                              gsem.at[slot]).wait()

    w = wts_ref[...]
    w1 = w[:, 0:1]
    w2 = w[:, 1:2]
    chunks = []
    for c in range(ROW_TILE):
        chunks.append(x_ref[:, c * LANES:(c + 1) * LANES]
                      + w1 * _load_tile_chunk(ybuf.at[slot, 0], c, TM_COMB)
                      + w2 * _load_tile_chunk(ybuf.at[slot, 1], c, TM_COMB))
    if fuse_norm:
        ss = chunks[0] * chunks[0]
        for ch in chunks[1:]:
            ss = ss + ch * ch
        inv = lax.rsqrt(jnp.sum(ss, axis=-1, keepdims=True) * (1.0 / D_MODEL) + EPS)
        fnw = fnw_ref[...]
        chunks = [ch * inv * fnw[:, c * LANES:(c + 1) * LANES] for c, ch in enumerate(chunks)]
    for c, ch in enumerate(chunks):
        o_ref[:, c * LANES:(c + 1) * LANES] = ch


def _combine(x, y, pos_tiles, wts, final_norm_w, fuse_norm):
    t = x.shape[0]
    return pl.pallas_call(
        functools.partial(_combine_kernel, fuse_norm),
        out_shape=jax.ShapeDtypeStruct((t, D_MODEL), F32),
        grid=(t // TM_COMB,),
        in_specs=[
            pl.BlockSpec(memory_space=pl.ANY),
            pl.BlockSpec(memory_space=pl.ANY),
            pl.BlockSpec((TM_COMB, D_MODEL), lambda i: (i, 0)),
            pl.BlockSpec((TM_COMB, LANES), lambda i: (i, 0)),
            pl.BlockSpec((1, D_MODEL), lambda i: (0, 0)),
        ],
        out_specs=pl.BlockSpec((TM_COMB, D_MODEL), lambda i: (i, 0)),
        scratch_shapes=[
            pltpu.SMEM((2 * TOP_K * TM_COMB,), I32),
            pltpu.VMEM((2, TOP_K, TM_COMB * ROW_TILE, LANES), F32),
            pltpu.SemaphoreType.DMA((2,)),
            pltpu.SemaphoreType.DMA((2,)),
        ],
        compiler_params=pltpu.CompilerParams(dimension_semantics=("arbitrary",)),
        name="moe_combine",
    )(pos_tiles, y, x, wts, final_norm_w)


def _tile_major(pos, tile):
    t = pos.shape[1]
    return pos.reshape(TOP_K, t // tile, tile).transpose(1, 0, 2).reshape(t // tile, TOP_K * tile)


def _moe_ffn(x, norm_w, router_w, w1, w3, w2, final_norm_w, fuse_norm):
    t = x.shape[0]
    h, meta, wts, cnt = _route(x, norm_w, router_w)

    counts = cnt[:, 0]
    padded = ((counts + TM_GRP - 1) // TM_GRP) * TM_GRP
    ends = jnp.cumsum(padded)
    offs = ends - padded
    n_sorted_rows = TOP_K * t + N_EXPERTS * TM_GRP
    tile_start = jnp.arange(n_sorted_rows // TM_GRP, dtype=I32) * TM_GRP
    tile_expert = jnp.minimum(
        jnp.sum((tile_start[:, None] >= ends[None, :]).astype(I32), axis=1), N_EXPERTS - 1)
    n_valid = (ends[-1] // TM_GRP).astype(I32).reshape(1)
    pos = (jnp.stack([offs[meta[0]] + meta[2], offs[meta[1]] + meta[3]]) * ROW_TILE).astype(I32)
    zero_tiles = jnp.concatenate([ends // TM_GRP - 1, padded, n_valid]).astype(I32)

    xs = _dispatch(h, _tile_major(pos, TM_DISP), zero_tiles, n_sorted_rows)
    y = _experts(xs, tile_expert.astype(I32), n_valid, w1, w3, w2)
    return _combine(x, y, _tile_major(pos, TM_COMB), wts, final_norm_w, fuse_norm)


def _final_norm_kernel(x_ref, nw_ref, o_ref):
    o_ref[...] = _rms_norm_rows(x_ref[...], nw_ref[...])


def _final_norm(x, norm_w):
    t = x.shape[0]
    return pl.pallas_call(
        _final_norm_kernel,
        out_shape=jax.ShapeDtypeStruct((t, D_MODEL), F32),
        grid=(t // TM_PROJ,),
        in_specs=[
            pl.BlockSpec((TM_PROJ, D_MODEL), lambda i: (i, 0)),
            pl.BlockSpec((1, D_MODEL), lambda i: (0, 0)),
        ],
        out_specs=pl.BlockSpec((TM_PROJ, D_MODEL), lambda i: (i, 0)),
        compiler_params=pltpu.CompilerParams(dimension_semantics=("arbitrary",)),
        name="final_norm",
    )(x, norm_w)


def kernel(x, w_in, lower_bounds, hgrn_norm_w, conv_w, w_proj_hgrn, w_proj_conv, w_out,
           norm_mix, norm_ffn, dense_w1, dense_w3, dense_w2, router_w,
           expert_w1, expert_w3, expert_w2, final_norm):
    batch, seq, d = x.shape
    depth = w_in.shape[0]
    assert d == D_MODEL and seq % CT_REC == 0 and (batch * seq) % TM_PROJ == 0
    assert w_in.shape[2] == N_GROUPS * D_MODEL

    xt = x.reshape(batch * seq, d)
    tri = jnp.tril(jnp.ones((CHUNK, CHUNK), F32))
    lower_bounds = lower_bounds.astype(F32)
    final_w = final_norm.reshape(1, d)
    normed = False

    for layer in range(depth):
        proj = _inproj(xt, norm_mix[layer].reshape(1, d), w_in[layer].astype(BF16))
        o_gated = _recurrence(proj, lower_bounds, hgrn_norm_w[layer].reshape(1, HEAD_DIM),
                              tri, layer, batch, seq)
        xt = _mixout(xt, o_gated, proj, conv_w[layer],
                     w_proj_hgrn[layer].astype(BF16), w_proj_conv[layer].astype(BF16),
                     w_out[layer].astype(BF16), seq)
        j = layer // 2
        nw = norm_ffn[layer].reshape(1, d)
        if layer % 2 == 0:
            xt = _dense_ffn(xt, nw, dense_w1[j].astype(BF16), dense_w3[j].astype(BF16),
                            dense_w2[j].astype(BF16))
        else:
            normed = layer == depth - 1
            xt = _moe_ffn(xt, nw, router_w[j].astype(F32), expert_w1[j].astype(BF16),
                          expert_w3[j].astype(BF16), expert_w2[j].astype(BF16),
                          final_w, normed)
    out = xt if normed else _final_norm(xt, final_w)
    return out.reshape(batch, seq, d)
```

```python
import functools

import jax
import jax.numpy as jnp
from jax import lax
from jax.experimental import pallas as pl
from jax.experimental.pallas import tpu as pltpu

D_MODEL = 1024
HEADS = 8
HEAD_DIM = 128
HGRN_SCALE = HEAD_DIM ** -0.5
F_MIN = 1e-6
CONV_K = 3
D_FF = 2816
N_EXPERTS = 8
TOP_K = 2
EPS = 1e-6
LOG2_E = 1.4426950408889634
N_GROUPS = 9
G_Q, G_F, G_I, G_G, G_B, G_C, G_U, G_GA, G_GB = range(N_GROUPS)

LANES = 128
SUBLANES = 8
ROW_TILE = SUBLANES
BF16_SUBLANES = 16
VMEM_LIMIT = 56 * 1024 * 1024

TM_PROJ = 512
TN_PROJ = 2304
TM_MIX = 512
TM_FFN = 512
MXU_TILE = 256
FF_SPLITS = (0, 6 * MXU_TILE, D_FF)
TM_ROUTE = 512
TM_DISP = 512
TM_GRP = 512
TM_COMB = 512
DMA_ISSUE_UNROLL = 8
ROW_GROUP = 64
CT_REC = 1024
CHUNK = 128
SUB = 32
N_SUB = CHUNK // SUB
MAX_EXP2 = 126.0
REC_UNROLL = 4

F32 = jnp.float32
BF16 = jnp.bfloat16
I32 = jnp.int32


def _rms_norm_rows(x, w):
    ms = jnp.mean(x * x, axis=-1, keepdims=True)
    return x * lax.rsqrt(ms + EPS) * w


def _exp_neg(x):
    return jnp.exp2(x * (-LOG2_E))


def _sigmoid(x):
    return 1.0 / (1.0 + _exp_neg(x))


def _silu(x):
    return x * _sigmoid(x)


def _split_bf16(v):
    hi = v.astype(BF16)
    lo = (v - hi.astype(F32)).astype(BF16)
    return hi, lo


def _dot3(a_hi, a_lo, b_hi, b_lo, dims):
    dg = functools.partial(lax.dot_general, dimension_numbers=dims,
                           preferred_element_type=F32)
    return dg(a_hi, b_hi) + dg(a_hi, b_lo) + dg(a_lo, b_hi)


def _inproj_kernel(x_ref, nw_ref, w_ref, o_ref):
    h = _rms_norm_rows(x_ref[...], nw_ref[...]).astype(BF16)
    for c in range(w_ref.shape[1] // TN_PROJ):
        cols = slice(c * TN_PROJ, (c + 1) * TN_PROJ)
        o_ref[:, cols] = jnp.dot(h, w_ref[:, cols],
                                 preferred_element_type=F32).astype(o_ref.dtype)


def _inproj(x, norm_w, w_in_bf16):
    t = x.shape[0]
    n = w_in_bf16.shape[1]
    assert n % TN_PROJ == 0
    return pl.pallas_call(
        _inproj_kernel,
        out_shape=jax.ShapeDtypeStruct((t, n), BF16),
        grid=(t // TM_PROJ,),
        in_specs=[
            pl.BlockSpec((TM_PROJ, D_MODEL), lambda i: (i, 0)),
            pl.BlockSpec((1, D_MODEL), lambda i: (0, 0)),
            pl.BlockSpec((D_MODEL, n), lambda i: (0, 0), pipeline_mode=pl.Buffered(1)),
        ],
        out_specs=pl.BlockSpec((TM_PROJ, n), lambda i: (i, 0)),
        compiler_params=pltpu.CompilerParams(
            dimension_semantics=("arbitrary",),
            vmem_limit_bytes=VMEM_LIMIT),
        name="inproj",
    )(x, norm_w, w_in_bf16)


def _layer_lower_bound(lb_raw, layer):
    m = jnp.max(lb_raw, axis=0, keepdims=True)
    e = jnp.exp(lb_raw - m)
    soft = e / jnp.sum(e, axis=0, keepdims=True)
    acc = soft[0:1, :]
    for j in range(1, layer + 1):
        acc = acc + soft[j:j + 1, :]
    return acc - soft[0:1, :]


def _rec_chunk(rows, slot, q_ref, f_ref, i_ref, g_ref, o_ref, st_ref, lb, gw, tri_f32, tri_bf,
               lhs_s, rhs_s, kst_s, sg_s):
    f = lb + (1.0 - lb) * _sigmoid(f_ref[rows, :].astype(F32))
    logf = jnp.log2(jnp.maximum(f, F_MIN))
    kk = 1.0 - f

    hi, lo = _split_bf16(logf)
    cum2 = jnp.dot(tri_bf, jnp.concatenate([hi, lo], axis=1), preferred_element_type=F32)
    cum = cum2[:, :D_MODEL] + cum2[:, D_MODEL:]

    s_ref = [jnp.zeros((1, D_MODEL), F32)]
    for j in range(1, N_SUB):
        s_ref.append(cum[j * SUB - 1:j * SUB, :])
    cum_end = cum[CHUNK - 1:CHUNK, :]
    decay_end = jnp.exp2(cum_end)

    qh = _silu(q_ref[rows, :].astype(F32))
    for j in range(N_SUB):
        blk = slice(j * SUB, (j + 1) * SUB)
        q_loc = qh[blk] * jnp.exp2(cum[blk] - s_ref[j])
        k_loc = kk[blk] * jnp.exp2(jnp.minimum(s_ref[j] - cum[blk], MAX_EXP2))
        rhs_s[slot, blk, j * D_MODEL:(j + 1) * D_MODEL] = k_loc.astype(BF16)
        kst_s[slot, blk, :] = (k_loc * jnp.exp2(cum_end - s_ref[j])).astype(BF16)
        for jj in range(j + 1):
            val = q_loc if jj == j else q_loc * jnp.exp2(s_ref[j] - s_ref[jj])
            lhs_s[slot, blk, jj * D_MODEL:(jj + 1) * D_MODEL] = val.astype(BF16)
    sg_s[slot] = _silu(g_ref[rows, :].astype(F32))

    for h in range(HEADS):
        cols = slice(h * HEAD_DIM, (h + 1) * HEAD_DIM)
        seg_cols = [slice(j * D_MODEL + h * HEAD_DIM, j * D_MODEL + (h + 1) * HEAD_DIM)
                    for j in range(N_SUB)]
        lhs = jnp.concatenate([lhs_s[slot, :, sc] for sc in seg_cols], axis=1)
        rhs = jnp.concatenate([rhs_s[slot, :, sc] for sc in seg_cols], axis=1)
        scores = lax.dot_general(lhs, rhs, (((1,), (1,)), ((), ())),
                                 preferred_element_type=F32) * tri_f32
        v_bf = i_ref[rows, cols]
        st = st_ref[h]
        o = (jnp.dot(scores.astype(BF16), v_bf, preferred_element_type=F32)
             + lax.dot_general(lhs_s[slot, :, seg_cols[0]], st.astype(BF16),
                               (((1,), (1,)), ((), ())), preferred_element_type=F32))
        st_ref[h] = st * decay_end[:, cols] + lax.dot_general(
            v_bf, kst_s[slot, :, cols], (((0,), (0,)), ((), ())), preferred_element_type=F32)
        ms = jnp.mean(o * o, axis=-1, keepdims=True)
        out = o * lax.rsqrt(ms + EPS / (HGRN_SCALE * HGRN_SCALE)) * gw * sg_s[slot, :, cols]
        o_ref[rows, cols] = out.astype(o_ref.dtype)


def _rec_kernel(layer, q_ref, f_ref, i_ref, g_ref, lbraw_ref, gw_ref, tri_ref,
                 o_ref, st_ref, lhs_s, rhs_s, kst_s, sg_s):
    @pl.when(pl.program_id(1) == 0)
    def _():
        st_ref[...] = jnp.zeros_like(st_ref)
        lhs_s[...] = jnp.zeros_like(lhs_s)
        rhs_s[...] = jnp.zeros_like(rhs_s)

    lb = _layer_lower_bound(lbraw_ref[...], layer)
    gw = gw_ref[...]
    tri_f32 = tri_ref[...]
    tri_bf = tri_f32.astype(BF16)

    def body(cp, carry):
        for u in range(REC_UNROLL):
            r0 = pl.multiple_of((cp * REC_UNROLL + u) * CHUNK, CHUNK)
            _rec_chunk(pl.ds(r0, CHUNK), u, q_ref, f_ref, i_ref, g_ref, o_ref, st_ref,
                       lb, gw, tri_f32, tri_bf, lhs_s, rhs_s, kst_s, sg_s)
        return carry

    lax.fori_loop(0, CT_REC // (CHUNK * REC_UNROLL), body, 0)


def _recurrence(proj, lower_bounds, gnorm_w, tri, layer, batch, seq):
    t = proj.shape[0]
    steps = seq // CT_REC

    def group_spec(g):
        return pl.BlockSpec((CT_REC, D_MODEL), lambda b, s: (b * steps + s, g))

    return pl.pallas_call(
        functools.partial(_rec_kernel, layer),
        out_shape=jax.ShapeDtypeStruct((t, D_MODEL), BF16),
        grid=(batch, steps),
        in_specs=[
            group_spec(G_Q), group_spec(G_F), group_spec(G_I), group_spec(G_G),
            pl.BlockSpec(lower_bounds.shape, lambda b, s: (0, 0)),
            pl.BlockSpec((1, HEAD_DIM), lambda b, s: (0, 0)),
            pl.BlockSpec((CHUNK, CHUNK), lambda b, s: (0, 0)),
        ],
        out_specs=pl.BlockSpec((CT_REC, D_MODEL), lambda b, s: (b * steps + s, 0)),
        scratch_shapes=[
            pltpu.VMEM((HEADS, HEAD_DIM, HEAD_DIM), F32),
            pltpu.VMEM((REC_UNROLL, CHUNK, N_SUB * D_MODEL), BF16),
            pltpu.VMEM((REC_UNROLL, CHUNK, N_SUB * D_MODEL), BF16),
            pltpu.VMEM((REC_UNROLL, CHUNK, D_MODEL), BF16),
            pltpu.VMEM((REC_UNROLL, CHUNK, D_MODEL), F32),
        ],
        compiler_params=pltpu.CompilerParams(
            dimension_semantics=("arbitrary", "arbitrary"),
            vmem_limit_bytes=VMEM_LIMIT),
        name="hgrn2_recurrence",
    )(proj, proj, proj, proj, lower_bounds, gnorm_w, tri)


def _mixout_kernel(seq, x_ref, oa_ref, b_ref, c_ref, u_ref, ga_ref, gb_ref,
                   bh_ref, uh_ref, cw_ref, wph_ref, wpc_ref, wo_ref, o_ref):
    i = pl.program_id(0)
    v = b_ref[...].astype(F32) * u_ref[...].astype(F32)
    at_seq_start = (i * TM_MIX) % seq == 0
    halo = bh_ref[...].astype(F32) * uh_ref[...].astype(F32)
    halo = jnp.where(at_seq_start, 0.0, halo)
    prev1 = halo[BF16_SUBLANES - 1:BF16_SUBLANES, :]
    prev2 = halo[BF16_SUBLANES - 2:BF16_SUBLANES - 1, :]
    row = lax.broadcasted_iota(I32, v.shape, 0)
    v1 = jnp.where(row == 0, prev1, pltpu.roll(v, 1, 0))
    v2 = jnp.where(row == 0, prev2,
                   jnp.where(row == 1, prev1, pltpu.roll(v, 2, 0)))
    cw = cw_ref[...]
    conv = cw[0:1, :] * v2 + cw[1:2, :] * v1 + cw[2:3, :] * v
    yb_in = (c_ref[...].astype(F32) * conv).astype(BF16)
    y_b = jnp.dot(yb_in, wpc_ref[...], preferred_element_type=F32)
    y_a = jnp.dot(oa_ref[...], wph_ref[...], preferred_element_type=F32)
    merged = (_sigmoid(ga_ref[...].astype(F32)) * y_a
              + _sigmoid(gb_ref[...].astype(F32)) * y_b).astype(BF16)
    o_ref[...] = x_ref[...] + jnp.dot(merged, wo_ref[...], preferred_element_type=F32)


def _mixout(x, o_gated, proj, conv_w, wph, wpc, wo, seq):
    t = x.shape[0]
    halo_per_tile = TM_MIX // BF16_SUBLANES

    def group_spec(g):
        return pl.BlockSpec((TM_MIX, D_MODEL), lambda i: (i, g))

    def halo_spec(g):
        return pl.BlockSpec(
            (BF16_SUBLANES, D_MODEL),
            lambda i: (jnp.maximum(i * halo_per_tile - 1, 0), g))

    def weight_spec():
        return pl.BlockSpec((D_MODEL, D_MODEL), lambda i: (0, 0))

    return pl.pallas_call(
        functools.partial(_mixout_kernel, seq),
        out_shape=jax.ShapeDtypeStruct((t, D_MODEL), F32),
        grid=(t // TM_MIX,),
        in_specs=[
            pl.BlockSpec((TM_MIX, D_MODEL), lambda i: (i, 0)),
            pl.BlockSpec((TM_MIX, D_MODEL), lambda i: (i, 0)),
            group_spec(G_B), group_spec(G_C), group_spec(G_U),
            group_spec(G_GA), group_spec(G_GB),
            halo_spec(G_B), halo_spec(G_U),
            pl.BlockSpec((CONV_K, D_MODEL), lambda i: (0, 0)),
            weight_spec(), weight_spec(), weight_spec(),
        ],
        out_specs=pl.BlockSpec((TM_MIX, D_MODEL), lambda i: (i, 0)),
        compiler_params=pltpu.CompilerParams(
            dimension_semantics=("arbitrary",),
            vmem_limit_bytes=VMEM_LIMIT),
        name="mixer_out",
    )(x, o_gated, proj, proj, proj, proj, proj, proj, proj, conv_w, wph, wpc, wo)


def _dense_ffn_kernel(x_ref, nw_ref, w1_ref, w3_ref, w2_ref, o_ref):
    x = x_ref[...]
    h = _rms_norm_rows(x, nw_ref[...]).astype(BF16)
    a = jnp.dot(h, w1_ref[...], preferred_element_type=F32)
    b = jnp.dot(h, w3_ref[...], preferred_element_type=F32)
    act = (_silu(a) * b).astype(BF16)
    o_ref[...] = x + jnp.dot(act, w2_ref[...], preferred_element_type=F32)


def _dense_ffn(x, norm_w, w1, w3, w2):
    t = x.shape[0]
    return pl.pallas_call(
        _dense_ffn_kernel,
        out_shape=jax.ShapeDtypeStruct((t, D_MODEL), F32),
        grid=(t // TM_FFN,),
        in_specs=[
            pl.BlockSpec((TM_FFN, D_MODEL), lambda i: (i, 0)),
            pl.BlockSpec((1, D_MODEL), lambda i: (0, 0)),
            pl.BlockSpec((D_MODEL, D_FF), lambda i: (0, 0)),
            pl.BlockSpec((D_MODEL, D_FF), lambda i: (0, 0)),
            pl.BlockSpec((D_FF, D_MODEL), lambda i: (0, 0)),
        ],
        out_specs=pl.BlockSpec((TM_FFN, D_MODEL), lambda i: (i, 0)),
        compiler_params=pltpu.CompilerParams(
            dimension_semantics=("arbitrary",),
            vmem_limit_bytes=VMEM_LIMIT),
        name="dense_ffn",
    )(x, norm_w, w1, w3, w2)


def _route_kernel(x_ref, nw_ref, rwp_ref, triu_ref,
                  h_ref, meta_ref, wts_ref, cnt_ref, carry_scr):
    @pl.when(pl.program_id(0) == 0)
    def _():
        carry_scr[...] = jnp.zeros_like(carry_scr)

    h = _rms_norm_rows(x_ref[...], nw_ref[...])
    h_ref[...] = h
    h_hi, h_lo = _split_bf16(h)
    rwp_hi, rwp_lo = _split_bf16(rwp_ref[...])
    lr = _dot3(h_hi, h_lo, rwp_hi, rwp_lo, (((1,), (0,)), ((), ())))

    lt = jnp.transpose(lr)[:N_EXPERTS, :]
    sub = lax.broadcasted_iota(I32, lt.shape, 0)
    m1 = jnp.max(lt, axis=0, keepdims=True)
    i1 = jnp.min(jnp.where(lt == m1, sub, N_EXPERTS), axis=0, keepdims=True)
    rest = jnp.where(sub == i1, -jnp.inf, lt)
    m2 = jnp.max(rest, axis=0, keepdims=True)
    i2 = jnp.min(jnp.where(rest == m2, sub, N_EXPERTS), axis=0, keepdims=True)
    oh1 = sub == i1
    oh2 = sub == i2
    member = jnp.where(jnp.logical_or(oh1, oh2), 1.0, 0.0)
    prefix = jnp.dot(member.astype(BF16), triu_ref[...], preferred_element_type=F32)
    carry = carry_scr[...]
    base = carry[:, 0:1] + prefix
    rank1 = jnp.sum(jnp.where(oh1, base, 0.0), axis=0, keepdims=True)
    rank2 = jnp.sum(jnp.where(oh2, base, 0.0), axis=0, keepdims=True)
    carry_new = carry + jnp.sum(member, axis=1, keepdims=True)
    carry_scr[...] = carry_new
    cnt_ref[...] = carry_new.astype(I32)
    meta_ref[...] = jnp.where(
        sub == 0, i1, jnp.where(
            sub == 1, i2, jnp.where(
                sub == 2, rank1.astype(I32), jnp.where(
                    sub == 3, rank2.astype(I32), 0))))

    lane = lax.broadcasted_iota(I32, lr.shape, 1)
    lr = jnp.where(lane < N_EXPERTS, lr, -jnp.inf)
    m1r = jnp.max(lr, axis=-1, keepdims=True)
    i1r = jnp.min(jnp.where(lr == m1r, lane, LANES), axis=-1, keepdims=True)
    m2r = jnp.max(jnp.where(lane == i1r, -jnp.inf, lr), axis=-1, keepdims=True)
    e2 = jnp.exp(m2r - m1r)
    w1 = 1.0 / (1.0 + e2)
    w2 = e2 * w1
    wts_ref[...] = jnp.where(lane == 0, w1, jnp.where(lane == 1, w2, 0.0))


def _route(x, norm_w, router_w):
    t = x.shape[0]
    rwp = jnp.pad(router_w, ((0, 0), (0, LANES - N_EXPERTS)))
    triu = jnp.triu(jnp.ones((TM_ROUTE, TM_ROUTE), BF16), k=1)
    return pl.pallas_call(
        _route_kernel,
        out_shape=(
            jax.ShapeDtypeStruct((t, D_MODEL), F32),
            jax.ShapeDtypeStruct((SUBLANES, t), I32),
            jax.ShapeDtypeStruct((t, LANES), F32),
            jax.ShapeDtypeStruct((N_EXPERTS, LANES), I32),
        ),
        grid=(t // TM_ROUTE,),
        in_specs=[
            pl.BlockSpec((TM_ROUTE, D_MODEL), lambda i: (i, 0)),
            pl.BlockSpec((1, D_MODEL), lambda i: (0, 0)),
            pl.BlockSpec((D_MODEL, LANES), lambda i: (0, 0)),
            pl.BlockSpec((TM_ROUTE, TM_ROUTE), lambda i: (0, 0)),
        ],
        out_specs=(
            pl.BlockSpec((TM_ROUTE, D_MODEL), lambda i: (i, 0)),
            pl.BlockSpec((SUBLANES, TM_ROUTE), lambda i: (0, i)),
            pl.BlockSpec((TM_ROUTE, LANES), lambda i: (i, 0)),
            pl.BlockSpec((N_EXPERTS, LANES), lambda i: (0, 0)),
        ),
        scratch_shapes=[pltpu.VMEM((N_EXPERTS, LANES), F32)],
        compiler_params=pltpu.CompilerParams(
            dimension_semantics=("arbitrary",),
            vmem_limit_bytes=VMEM_LIMIT),
        name="moe_route",
    )(x, norm_w, rwp, triu)


def _store_rows_as_tiles(ref, value):
    n = value.shape[0]
    for c in range(ROW_TILE):
        ref[pl.ds(c, n, stride=ROW_TILE), :] = value[:, c * LANES:(c + 1) * LANES]


def _load_tile_chunk(ref, c, n):
    return ref[pl.ds(c, n, stride=ROW_TILE), :]


def _pos_copy(pos_hbm, pos_smem, sem, step, slot):
    n = pos_hbm.shape[1]
    dst = pos_smem.at[pl.ds(pl.multiple_of(slot * n, n), n)]
    return pltpu.make_async_copy(pos_hbm.at[step], dst, sem.at[slot])


def _dispatch_kernel(zt_ref, pos_hbm, h_ref, xs_hbm, pos_smem, hbuf, zbuf, psem, dsem, zsem):
    i = pl.program_id(0)
    n = pl.num_programs(0)
    slot = i % 2

    @pl.when(i == 0)
    def _():
        _pos_copy(pos_hbm, pos_smem, psem, 0, 0).start()

    _pos_copy(pos_hbm, pos_smem, psem, i, slot).wait()

    @pl.when(i + 1 < n)
    def _():
        _pos_copy(pos_hbm, pos_smem, psem, i + 1, 1 - slot).start()

    def wait_rows_from(s):
        for _ in range(TOP_K):
            pltpu.make_async_copy(hbuf.at[s], xs_hbm.at[pl.ds(0, TM_DISP * ROW_TILE)],
                                  dsem.at[s]).wait()

    @pl.when(i == 0)
    def _():
        zbuf[...] = jnp.zeros_like(zbuf)
        n_tiles = xs_hbm.shape[0] // (TM_GRP * ROW_TILE)

        def zero_tile_copy(k):
            r0 = pl.multiple_of(k * (TM_GRP * ROW_TILE), TM_GRP * ROW_TILE)
            return pltpu.make_async_copy(zbuf, xs_hbm.at[pl.ds(r0, TM_GRP * ROW_TILE)], zsem)

        def for_each_zero_tile(fn):
            for e in range(N_EXPERTS):
                @pl.when(zt_ref[N_EXPERTS + e] > 0)
                def _():
                    fn(zero_tile_copy(zt_ref[e]))

                @pl.when(zt_ref[2 * N_EXPERTS] + e < n_tiles)
                def _():
                    fn(zero_tile_copy(zt_ref[2 * N_EXPERTS] + e))

        for_each_zero_tile(lambda cp: cp.start())
        for_each_zero_tile(lambda cp: cp.wait())

    @pl.when(i >= 2)
    def _():
        wait_rows_from(slot)

    pos_base = slot * (TOP_K * TM_DISP)
    ring = hbuf.at[slot]

    def move_group(g, carry):
        r0 = pl.multiple_of(g * ROW_GROUP, ROW_GROUP)
        t0 = pl.multiple_of(g * (ROW_GROUP * ROW_TILE), ROW_GROUP * ROW_TILE)
        for c in range(ROW_TILE):
            ring[pl.ds(t0 + c, ROW_GROUP, stride=ROW_TILE), :] = (
                h_ref[pl.ds(r0, ROW_GROUP), c * LANES:(c + 1) * LANES])
        for r in range(ROW_GROUP):
            src = ring.at[pl.ds(pl.multiple_of(t0 + r * ROW_TILE, ROW_TILE), ROW_TILE)]
            for j in range(TOP_K):
                p = pl.multiple_of(pos_smem[pos_base + j * TM_DISP + r0 + r], ROW_TILE)
                pltpu.make_async_copy(src, xs_hbm.at[pl.ds(p, ROW_TILE)],
                                      dsem.at[slot]).start(priority=j)
        return carry

    lax.fori_loop(0, TM_DISP // ROW_GROUP, move_group, 0)

    @pl.when(i == n - 1)
    def _():
        wait_rows_from(slot)

        @pl.when(n > 1)
        def _():
            wait_rows_from(1 - slot)


def _dispatch(h, pos_tiles, zero_tiles, n_sorted_rows):
    t = h.shape[0]
    return pl.pallas_call(
        _dispatch_kernel,
        out_shape=jax.ShapeDtypeStruct((n_sorted_rows * ROW_TILE, LANES), F32),
        grid_spec=pltpu.PrefetchScalarGridSpec(
            num_scalar_prefetch=1,
            grid=(t // TM_DISP,),
            in_specs=[pl.BlockSpec(memory_space=pl.ANY),
                      pl.BlockSpec((TM_DISP, D_MODEL), lambda i, zt: (i, 0))],
            out_specs=pl.BlockSpec(memory_space=pl.ANY),
            scratch_shapes=[
                pltpu.SMEM((2 * TOP_K * TM_DISP,), I32),
                pltpu.VMEM((2, TM_DISP * ROW_TILE, LANES), F32),
                pltpu.VMEM((TM_GRP * ROW_TILE, LANES), F32),
                pltpu.SemaphoreType.DMA((2,)),
                pltpu.SemaphoreType.DMA((2,)),
                pltpu.SemaphoreType.DMA(()),
            ]),
        compiler_params=pltpu.CompilerParams(
            dimension_semantics=("arbitrary",),
            vmem_limit_bytes=VMEM_LIMIT),
        name="moe_dispatch",
    )(zero_tiles, pos_tiles, h)


def _experts_kernel(te_ref, nv_ref, xs_ref, w1_ref, w3_ref, w2_ref, y_ref, xb_scr):
    k = pl.program_id(0)

    @pl.when(k < nv_ref[0])
    def _():
        for c in range(ROW_TILE):
            xb_scr[:, c * LANES:(c + 1) * LANES] = _load_tile_chunk(
                xs_ref, c, TM_GRP).astype(BF16)
        xb = xb_scr[...]
        y = None
        for lo, hi in zip(FF_SPLITS[:-1], FF_SPLITS[1:]):
            cols = slice(lo, hi)
            a = jnp.dot(xb, w1_ref[0, :, cols], preferred_element_type=F32)
            b = jnp.dot(xb, w3_ref[0, :, cols], preferred_element_type=F32)
            act = (_silu(a) * b).astype(BF16)
            part = jnp.dot(act, w2_ref[0, cols, :], preferred_element_type=F32)
            y = part if y is None else y + part
        _store_rows_as_tiles(y_ref, y)

    @pl.when(k >= nv_ref[0])
    def _():
        y_ref[...] = jnp.zeros_like(y_ref)


def _experts(xs, tile_expert, n_valid, w1, w3, w2):
    n_rows = xs.shape[0] // ROW_TILE

    def row_map(k, te, nv):
        return (jnp.minimum(k, nv[0] - 1), 0)

    def out_map(k, te, nv):
        return (k, 0)

    def w_map(k, te, nv):
        return (te[jnp.minimum(k, nv[0] - 1)], 0, 0)

    def weight_spec(shape):
        return pl.BlockSpec((1,) + shape, w_map, pipeline_mode=pl.Buffered(1))

    return pl.pallas_call(
        _experts_kernel,
        out_shape=jax.ShapeDtypeStruct((n_rows * ROW_TILE, LANES), F32),
        grid_spec=pltpu.PrefetchScalarGridSpec(
            num_scalar_prefetch=2,
            grid=(n_rows // TM_GRP,),
            in_specs=[
                pl.BlockSpec((TM_GRP * ROW_TILE, LANES), row_map),
                weight_spec((D_MODEL, D_FF)),
                weight_spec((D_MODEL, D_FF)),
                weight_spec((D_FF, D_MODEL)),
            ],
            out_specs=pl.BlockSpec((TM_GRP * ROW_TILE, LANES), out_map),
            scratch_shapes=[pltpu.VMEM((TM_GRP, D_MODEL), BF16)]),
        compiler_params=pltpu.CompilerParams(
            dimension_semantics=("arbitrary",),
            vmem_limit_bytes=VMEM_LIMIT),
        name="moe_experts",
    )(tile_expert, n_valid, xs, w1, w3, w2)


def _combine_kernel(fuse_norm, pos_hbm, y_hbm, x_ref, wts_ref, fnw_ref, o_ref,
                    pos_smem, ybuf, psem, gsem):
    i = pl.program_id(0)
    n = pl.num_programs(0)
    slot = i % 2

    def issue_gathers(s):
        pos_base = s * (TOP_K * TM_COMB)

        def body(t, carry):
            dst_rows = pl.ds(pl.multiple_of(t * ROW_TILE, ROW_TILE), ROW_TILE)
            for j in range(TOP_K):
                p = pl.multiple_of(pos_smem[pos_base + j * TM_COMB + t], ROW_TILE)
                pltpu.make_async_copy(y_hbm.at[pl.ds(p, ROW_TILE)],
                                      ybuf.at[s, j, dst_rows], gsem.at[s]).start(priority=j)
            return carry

        lax.fori_loop(0, TM_COMB, body, 0, unroll=DMA_ISSUE_UNROLL)

    @pl.when(i == 0)
    def _():
        first = _pos_copy(pos_hbm, pos_smem, psem, 0, 0)
        first.start()
        first.wait()
        issue_gathers(0)

        @pl.when(n > 1)
        def _():
            _pos_copy(pos_hbm, pos_smem, psem, 1, 1).start()

    @pl.when(i + 1 < n)
    def _():
        _pos_copy(pos_hbm, pos_smem, psem, i + 1, 1 - slot).wait()
        issue_gathers(1 - slot)

        @pl.when(i + 2 < n)
        def _():
            _pos_copy(pos_hbm, pos_smem, psem, i + 2, slot).start()

    for j in range(TOP_K):
        pltpu.make_async_copy(y_hbm.at[pl.ds(0, TM_COMB * ROW_TILE)], ybuf.at[slot, j],
                              gsem.at[slot]).wait()

    w = wts_ref[...]
    w1 = w[:, 0:1]
    w2 = w[:, 1:2]
    chunks = []
    for c in range(ROW_TILE):
        chunks.append(x_ref[:, c * LANES:(c + 1) * LANES]
                      + w1 * _load_tile_chunk(ybuf.at[slot, 0], c, TM_COMB)
                      + w2 * _load_tile_chunk(ybuf.at[slot, 1], c, TM_COMB))
    if fuse_norm:
        ss = chunks[0] * chunks[0]
        for ch in chunks[1:]:
            ss = ss + ch * ch
        inv = lax.rsqrt(jnp.sum(ss, axis=-1, keepdims=True) * (1.0 / D_MODEL) + EPS)
        fnw = fnw_ref[...]
        chunks = [ch * inv * fnw[:, c * LANES:(c + 1) * LANES] for c, ch in enumerate(chunks)]
    for c, ch in enumerate(chunks):
        o_ref[:, c * LANES:(c + 1) * LANES] = ch


def _combine(x, y, pos_tiles, wts, final_norm_w, fuse_norm):
    t = x.shape[0]
    return pl.pallas_call(
        functools.partial(_combine_kernel, fuse_norm),
        out_shape=jax.ShapeDtypeStruct((t, D_MODEL), F32),
        grid=(t // TM_COMB,),
        in_specs=[
            pl.BlockSpec(memory_space=pl.ANY),
            pl.BlockSpec(memory_space=pl.ANY),
            pl.BlockSpec((TM_COMB, D_MODEL), lambda i: (i, 0)),
            pl.BlockSpec((TM_COMB, LANES), lambda i: (i, 0)),
            pl.BlockSpec((1, D_MODEL), lambda i: (0, 0)),
        ],
        out_specs=pl.BlockSpec((TM_COMB, D_MODEL), lambda i: (i, 0)),
        scratch_shapes=[
            pltpu.SMEM((2 * TOP_K * TM_COMB,), I32),
            pltpu.VMEM((2, TOP_K, TM_COMB * ROW_TILE, LANES), F32),
            pltpu.SemaphoreType.DMA((2,)),
            pltpu.SemaphoreType.DMA((2,)),
        ],
        compiler_params=pltpu.CompilerParams(dimension_semantics=("arbitrary",)),
        name="moe_combine",
    )(pos_tiles, y, x, wts, final_norm_w)


def _tile_major(pos, tile):
    t = pos.shape[1]
    return pos.reshape(TOP_K, t // tile, tile).transpose(1, 0, 2).reshape(t // tile, TOP_K * tile)


def _moe_ffn(x, norm_w, router_w, w1, w3, w2, final_norm_w, fuse_norm):
    t = x.shape[0]
    h, meta, wts, cnt = _route(x, norm_w, router_w)

    counts = cnt[:, 0]
    padded = ((counts + TM_GRP - 1) // TM_GRP) * TM_GRP
    ends = jnp.cumsum(padded)
    offs = ends - padded
    n_sorted_rows = TOP_K * t + N_EXPERTS * TM_GRP
    tile_start = jnp.arange(n_sorted_rows // TM_GRP, dtype=I32) * TM_GRP
    tile_expert = jnp.minimum(
        jnp.sum((tile_start[:, None] >= ends[None, :]).astype(I32), axis=1), N_EXPERTS - 1)
    n_valid = (ends[-1] // TM_GRP).astype(I32).reshape(1)
    pos = (jnp.stack([offs[meta[0]] + meta[2], offs[meta[1]] + meta[3]]) * ROW_TILE).astype(I32)
    zero_tiles = jnp.concatenate([ends // TM_GRP - 1, padded, n_valid]).astype(I32)

    xs = _dispatch(h, _tile_major(pos, TM_DISP), zero_tiles, n_sorted_rows)
    y = _experts(xs, tile_expert.astype(I32), n_valid, w1, w3, w2)
    return _combine(x, y, _tile_major(pos, TM_COMB), wts, final_norm_w, fuse_norm)


def _final_norm_kernel(x_ref, nw_ref, o_ref):
    o_ref[...] = _rms_norm_rows(x_ref[...], nw_ref[...])


def _final_norm(x, norm_w):
    t = x.shape[0]
    return pl.pallas_call(
        _final_norm_kernel,
        out_shape=jax.ShapeDtypeStruct((t, D_MODEL), F32),
        grid=(t // TM_PROJ,),
        in_specs=[
            pl.BlockSpec((TM_PROJ, D_MODEL), lambda i: (i, 0)),
            pl.BlockSpec((1, D_MODEL), lambda i: (0, 0)),
        ],
        out_specs=pl.BlockSpec((TM_PROJ, D_MODEL), lambda i: (i, 0)),
        compiler_params=pltpu.CompilerParams(dimension_semantics=("arbitrary",)),
        name="final_norm",
    )(x, norm_w)


def kernel(x, w_in, lower_bounds, hgrn_norm_w, conv_w, w_proj_hgrn, w_proj_conv, w_out,
           norm_mix, norm_ffn, dense_w1, dense_w3, dense_w2, router_w,
           expert_w1, expert_w3, expert_w2, final_norm):
    batch, seq, d = x.shape
    depth = w_in.shape[0]
    assert d == D_MODEL and seq % CT_REC == 0 and (batch * seq) % TM_PROJ == 0
    assert w_in.shape[2] == N_GROUPS * D_MODEL

    xt = x.reshape(batch * seq, d)
    tri = jnp.tril(jnp.ones((CHUNK, CHUNK), F32))
    lower_bounds = lower_bounds.astype(F32)
    final_w = final_norm.reshape(1, d)
    normed = False

    for layer in range(depth):
        proj = _inproj(xt, norm_mix[layer].reshape(1, d), w_in[layer].astype(BF16))
        o_gated = _recurrence(proj, lower_bounds, hgrn_norm_w[layer].reshape(1, HEAD_DIM),
                              tri, layer, batch, seq)
        xt = _mixout(xt, o_gated, proj, conv_w[layer],
                     w_proj_hgrn[layer].astype(BF16), w_proj_conv[layer].astype(BF16),
                     w_out[layer].astype(BF16), seq)
        j = layer // 2
        nw = norm_ffn[layer].reshape(1, d)
        if layer % 2 == 0:
            xt = _dense_ffn(xt, nw, dense_w1[j].astype(BF16), dense_w3[j].astype(BF16),
                            dense_w2[j].astype(BF16))
        else:
            normed = layer == depth - 1
            xt = _moe_ffn(xt, nw, router_w[j].astype(F32), expert_w1[j].astype(BF16),
                          expert_w3[j].astype(BF16), expert_w2[j].astype(BF16),
                          final_w, normed)
    out = xt if normed else _final_norm(xt, final_w)
    return out.reshape(batch, seq, d)
```

```python
import functools

import jax
import jax.numpy as jnp
from jax import lax
from jax.experimental import pallas as pl
from jax.experimental.pallas import tpu as pltpu

D_MODEL = 1024
HEADS = 8
HEAD_DIM = 128
HGRN_SCALE = HEAD_DIM ** -0.5
F_MIN = 1e-6
CONV_K = 3
D_FF = 2816
N_EXPERTS = 8
TOP_K = 2
EPS = 1e-6
LOG2_E = 1.4426950408889634
N_GROUPS = 9
G_Q, G_F, G_I, G_G, G_B, G_C, G_U, G_GA, G_GB = range(N_GROUPS)

LANES = 128
SUBLANES = 8
ROW_TILE = SUBLANES
BF16_SUBLANES = 16
VMEM_LIMIT = 56 * 1024 * 1024

TM_PROJ = 512
TN_PROJ = 2304
TM_MIX = 512
TM_FFN = 512
MXU_TILE = 256
FF_SPLITS = (0, 6 * MXU_TILE, D_FF)
TM_ROUTE = 512
TM_DISP = 512
TM_GRP = 512
TM_COMB = 512
DMA_ISSUE_UNROLL = 8
ROW_GROUP = 64
CT_REC = 1024
CHUNK = 128
SUB = 32
N_SUB = CHUNK // SUB
MAX_EXP2 = 126.0
REC_UNROLL = 4

F32 = jnp.float32
BF16 = jnp.bfloat16
I32 = jnp.int32


def _rms_norm_rows(x, w):
    ms = jnp.mean(x * x, axis=-1, keepdims=True)
    return x * lax.rsqrt(ms + EPS) * w


def _exp_neg(x):
    return jnp.exp2(x * (-LOG2_E))


def _sigmoid(x):
    return 1.0 / (1.0 + _exp_neg(x))


def _silu(x):
    return x * _sigmoid(x)


def _split_bf16(v):
    hi = v.astype(BF16)
    lo = (v - hi.astype(F32)).astype(BF16)
    return hi, lo


def _dot3(a_hi, a_lo, b_hi, b_lo, dims):
    dg = functools.partial(lax.dot_general, dimension_numbers=dims,
                           preferred_element_type=F32)
    return dg(a_hi, b_hi) + dg(a_hi, b_lo) + dg(a_lo, b_hi)


def _inproj_kernel(x_ref, nw_ref, w_ref, o_ref):
    h = _rms_norm_rows(x_ref[...], nw_ref[...]).astype(BF16)
    for c in range(w_ref.shape[1] // TN_PROJ):
        cols = slice(c * TN_PROJ, (c + 1) * TN_PROJ)
        o_ref[:, cols] = jnp.dot(h, w_ref[:, cols],
                                 preferred_element_type=F32).astype(o_ref.dtype)


def _inproj(x, norm_w, w_in_bf16):
    t = x.shape[0]
    n = w_in_bf16.shape[1]
    assert n % TN_PROJ == 0
    return pl.pallas_call(
        _inproj_kernel,
        out_shape=jax.ShapeDtypeStruct((t, n), BF16),
        grid=(t // TM_PROJ,),
        in_specs=[
            pl.BlockSpec((TM_PROJ, D_MODEL), lambda i: (i, 0)),
            pl.BlockSpec((1, D_MODEL), lambda i: (0, 0)),
            pl.BlockSpec((D_MODEL, n), lambda i: (0, 0), pipeline_mode=pl.Buffered(1)),
        ],
        out_specs=pl.BlockSpec((TM_PROJ, n), lambda i: (i, 0)),
        compiler_params=pltpu.CompilerParams(
            dimension_semantics=("arbitrary",),
            vmem_limit_bytes=VMEM_LIMIT),
        name="inproj",
    )(x, norm_w, w_in_bf16)


def _layer_lower_bound(lb_raw, layer):
    m = jnp.max(lb_raw, axis=0, keepdims=True)
    e = jnp.exp(lb_raw - m)
    soft = e / jnp.sum(e, axis=0, keepdims=True)
    acc = soft[0:1, :]
    for j in range(1, layer + 1):
        acc = acc + soft[j:j + 1, :]
    return acc - soft[0:1, :]


def _rec_chunk(rows, slot, q_ref, f_ref, i_ref, g_ref, o_ref, st_ref, lb, gw, tri_f32, tri_bf,
               lhs_s, rhs_s, kst_s, sg_s):
    f = lb + (1.0 - lb) * _sigmoid(f_ref[rows, :].astype(F32))
    logf = jnp.log2(jnp.maximum(f, F_MIN))
    kk = 1.0 - f

    hi, lo = _split_bf16(logf)
    cum2 = jnp.dot(tri_bf, jnp.concatenate([hi, lo], axis=1), preferred_element_type=F32)
    cum = cum2[:, :D_MODEL] + cum2[:, D_MODEL:]

    s_ref = [jnp.zeros((1, D_MODEL), F32)]
    for j in range(1, N_SUB):
        s_ref.append(cum[j * SUB - 1:j * SUB, :])
    cum_end = cum[CHUNK - 1:CHUNK, :]
    decay_end = jnp.exp2(cum_end)

    qh = _silu(q_ref[rows, :].astype(F32))
    for j in range(N_SUB):
        blk = slice(j * SUB, (j + 1) * SUB)
        q_loc = qh[blk] * jnp.exp2(cum[blk] - s_ref[j])
        k_loc = kk[blk] * jnp.exp2(jnp.minimum(s_ref[j] - cum[blk], MAX_EXP2))
        rhs_s[slot, blk, j * D_MODEL:(j + 1) * D_MODEL] = k_loc.astype(BF16)
        kst_s[slot, blk, :] = (k_loc * jnp.exp2(cum_end - s_ref[j])).astype(BF16)
        for jj in range(j + 1):
            val = q_loc if jj == j else q_loc * jnp.exp2(s_ref[j] - s_ref[jj])
            lhs_s[slot, blk, jj * D_MODEL:(jj + 1) * D_MODEL] = val.astype(BF16)
    sg_s[slot] = _silu(g_ref[rows, :].astype(F32))

    for h in range(HEADS):
        cols = slice(h * HEAD_DIM, (h + 1) * HEAD_DIM)
        seg_cols = [slice(j * D_MODEL + h * HEAD_DIM, j * D_MODEL + (h + 1) * HEAD_DIM)
                    for j in range(N_SUB)]
        lhs = jnp.concatenate([lhs_s[slot, :, sc] for sc in seg_cols], axis=1)
        rhs = jnp.concatenate([rhs_s[slot, :, sc] for sc in seg_cols], axis=1)
        scores = lax.dot_general(lhs, rhs, (((1,), (1,)), ((), ())),
                                 preferred_element_type=F32) * tri_f32
        v_bf = i_ref[rows, cols]
        st = st_ref[h]
        o = (jnp.dot(scores.astype(BF16), v_bf, preferred_element_type=F32)
             + lax.dot_general(lhs_s[slot, :, seg_cols[0]], st.astype(BF16),
                               (((1,), (1,)), ((), ())), preferred_element_type=F32))
        st_ref[h] = st * decay_end[:, cols] + lax.dot_general(
            v_bf, kst_s[slot, :, cols], (((0,), (0,)), ((), ())), preferred_element_type=F32)
        ms = jnp.mean(o * o, axis=-1, keepdims=True)
        out = o * lax.rsqrt(ms + EPS / (HGRN_SCALE * HGRN_SCALE)) * gw * sg_s[slot, :, cols]
        o_ref[rows, cols] = out.astype(o_ref.dtype)


def _rec_kernel(layer, q_ref, f_ref, i_ref, g_ref, lbraw_ref, gw_ref, tri_ref,
                 o_ref, st_ref, lhs_s, rhs_s, kst_s, sg_s):
    @pl.when(pl.program_id(1) == 0)
    def _():
        st_ref[...] = jnp.zeros_like(st_ref)
        lhs_s[...] = jnp.zeros_like(lhs_s)
        rhs_s[...] = jnp.zeros_like(rhs_s)

    lb = _layer_lower_bound(lbraw_ref[...], layer)
    gw = gw_ref[...]
    tri_f32 = tri_ref[...]
    tri_bf = tri_f32.astype(BF16)

    def body(cp, carry):
        for u in range(REC_UNROLL):
            r0 = pl.multiple_of((cp * REC_UNROLL + u) * CHUNK, CHUNK)
            _rec_chunk(pl.ds(r0, CHUNK), u, q_ref, f_ref, i_ref, g_ref, o_ref, st_ref,
                       lb, gw, tri_f32, tri_bf, lhs_s, rhs_s, kst_s, sg_s)
        return carry

    lax.fori_loop(0, CT_REC // (CHUNK * REC_UNROLL), body, 0)


def _recurrence(proj, lower_bounds, gnorm_w, tri, layer, batch, seq):
    t = proj.shape[0]
    steps = seq // CT_REC

    def group_spec(g):
        return pl.BlockSpec((CT_REC, D_MODEL), lambda b, s: (b * steps + s, g))

    return pl.pallas_call(
        functools.partial(_rec_kernel, layer),
        out_shape=jax.ShapeDtypeStruct((t, D_MODEL), BF16),
        grid=(batch, steps),
        in_specs=[
            group_spec(G_Q), group_spec(G_F), group_spec(G_I), group_spec(G_G),
            pl.BlockSpec(lower_bounds.shape, lambda b, s: (0, 0)),
            pl.BlockSpec((1, HEAD_DIM), lambda b, s: (0, 0)),
            pl.BlockSpec((CHUNK, CHUNK), lambda b, s: (0, 0)),
        ],
        out_specs=pl.BlockSpec((CT_REC, D_MODEL), lambda b, s: (b * steps + s, 0)),
        scratch_shapes=[
            pltpu.VMEM((HEADS, HEAD_DIM, HEAD_DIM), F32),
            pltpu.VMEM((REC_UNROLL, CHUNK, N_SUB * D_MODEL), BF16),
            pltpu.VMEM((REC_UNROLL, CHUNK, N_SUB * D_MODEL), BF16),
            pltpu.VMEM((REC_UNROLL, CHUNK, D_MODEL), BF16),
            pltpu.VMEM((REC_UNROLL, CHUNK, D_MODEL), F32),
        ],
        compiler_params=pltpu.CompilerParams(
            dimension_semantics=("arbitrary", "arbitrary"),
            vmem_limit_bytes=VMEM_LIMIT),
        name="hgrn2_recurrence",
    )(proj, proj, proj, proj, lower_bounds, gnorm_w, tri)


def _mixout_kernel(seq, x_ref, oa_ref, b_ref, c_ref, u_ref, ga_ref, gb_ref,
                   bh_ref, uh_ref, cw_ref, wph_ref, wpc_ref, wo_ref, o_ref):
    i = pl.program_id(0)
    v = b_ref[...].astype(F32) * u_ref[...].astype(F32)
    at_seq_start = (i * TM_MIX) % seq == 0
    halo = bh_ref[...].astype(F32) * uh_ref[...].astype(F32)
    halo = jnp.where(at_seq_start, 0.0, halo)
    prev1 = halo[BF16_SUBLANES - 1:BF16_SUBLANES, :]
    prev2 = halo[BF16_SUBLANES - 2:BF16_SUBLANES - 1, :]
    row = lax.broadcasted_iota(I32, v.shape, 0)
    v1 = jnp.where(row == 0, prev1, pltpu.roll(v, 1, 0))
    v2 = jnp.where(row == 0, prev2,
                   jnp.where(row == 1, prev1, pltpu.roll(v, 2, 0)))
    cw = cw_ref[...]
    conv = cw[0:1, :] * v2 + cw[1:2, :] * v1 + cw[2:3, :] * v
    yb_in = (c_ref[...].astype(F32) * conv).astype(BF16)
    y_b = jnp.dot(yb_in, wpc_ref[...], preferred_element_type=F32)
    y_a = jnp.dot(oa_ref[...], wph_ref[...], preferred_element_type=F32)
    merged = (_sigmoid(ga_ref[...].astype(F32)) * y_a
              + _sigmoid(gb_ref[...].astype(F32)) * y_b).astype(BF16)
    o_ref[...] = x_ref[...] + jnp.dot(merged, wo_ref[...], preferred_element_type=F32)


def _mixout(x, o_gated, proj, conv_w, wph, wpc, wo, seq):
    t = x.shape[0]
    halo_per_tile = TM_MIX // BF16_SUBLANES

    def group_spec(g):
        return pl.BlockSpec((TM_MIX, D_MODEL), lambda i: (i, g))

    def halo_spec(g):
        return pl.BlockSpec(
            (BF16_SUBLANES, D_MODEL),
            lambda i: (jnp.maximum(i * halo_per_tile - 1, 0), g))

    def weight_spec():
        return pl.BlockSpec((D_MODEL, D_MODEL), lambda i: (0, 0))

    return pl.pallas_call(
        functools.partial(_mixout_kernel, seq),
        out_shape=jax.ShapeDtypeStruct((t, D_MODEL), F32),
        grid=(t // TM_MIX,),
        in_specs=[
            pl.BlockSpec((TM_MIX, D_MODEL), lambda i: (i, 0)),
            pl.BlockSpec((TM_MIX, D_MODEL), lambda i: (i, 0)),
            group_spec(G_B), group_spec(G_C), group_spec(G_U),
            group_spec(G_GA), group_spec(G_GB),
            halo_spec(G_B), halo_spec(G_U),
            pl.BlockSpec((CONV_K, D_MODEL), lambda i: (0, 0)),
            weight_spec(), weight_spec(), weight_spec(),
        ],
        out_specs=pl.BlockSpec((TM_MIX, D_MODEL), lambda i: (i, 0)),
        compiler_params=pltpu.CompilerParams(
            dimension_semantics=("arbitrary",),
            vmem_limit_bytes=VMEM_LIMIT),
        name="mixer_out",
    )(x, o_gated, proj, proj, proj, proj, proj, proj, proj, conv_w, wph, wpc, wo)


def _dense_ffn_kernel(x_ref, nw_ref, w1_ref, w3_ref, w2_ref, o_ref):
    x = x_ref[...]
    h = _rms_norm_rows(x, nw_ref[...]).astype(BF16)
    a = jnp.dot(h, w1_ref[...], preferred_element_type=F32)
    b = jnp.dot(h, w3_ref[...], preferred_element_type=F32)
    act = (_silu(a) * b).astype(BF16)
    o_ref[...] = x + jnp.dot(act, w2_ref[...], preferred_element_type=F32)


def _dense_ffn(x, norm_w, w1, w3, w2):
    t = x.shape[0]
    return pl.pallas_call(
        _dense_ffn_kernel,
        out_shape=jax.ShapeDtypeStruct((t, D_MODEL), F32),
        grid=(t // TM_FFN,),
        in_specs=[
            pl.BlockSpec((TM_FFN, D_MODEL), lambda i: (i, 0)),
            pl.BlockSpec((1, D_MODEL), lambda i: (0, 0)),
            pl.BlockSpec((D_MODEL, D_FF), lambda i: (0, 0)),
            pl.BlockSpec((D_MODEL, D_FF), lambda i: (0, 0)),
            pl.BlockSpec((D_FF, D_MODEL), lambda i: (0, 0)),
        ],
        out_specs=pl.BlockSpec((TM_FFN, D_MODEL), lambda i: (i, 0)),
        compiler_params=pltpu.CompilerParams(
            dimension_semantics=("arbitrary",),
            vmem_limit_bytes=VMEM_LIMIT),
        name="dense_ffn",
    )(x, norm_w, w1, w3, w2)


def _route_kernel(x_ref, nw_ref, rwp_ref, triu_ref,
                  h_ref, meta_ref, wts_ref, cnt_ref, carry_scr):
    @pl.when(pl.program_id(0) == 0)
    def _():
        carry_scr[...] = jnp.zeros_like(carry_scr)

    h = _rms_norm_rows(x_ref[...], nw_ref[...])
    h_ref[...] = h
    h_hi, h_lo = _split_bf16(h)
    rwp_hi, rwp_lo = _split_bf16(rwp_ref[...])
    lr = _dot3(h_hi, h_lo, rwp_hi, rwp_lo, (((1,), (0,)), ((), ())))

    lt = jnp.transpose(lr)[:N_EXPERTS, :]
    sub = lax.broadcasted_iota(I32, lt.shape, 0)
    m1 = jnp.max(lt, axis=0, keepdims=True)
    i1 = jnp.min(jnp.where(lt == m1, sub, N_EXPERTS), axis=0, keepdims=True)
    rest = jnp.where(sub == i1, -jnp.inf, lt)
    m2 = jnp.max(rest, axis=0, keepdims=True)
    i2 = jnp.min(jnp.where(rest == m2, sub, N_EXPERTS), axis=0, keepdims=True)
    oh1 = sub == i1
    oh2 = sub == i2
    member = jnp.where(jnp.logical_or(oh1, oh2), 1.0, 0.0)
    prefix = jnp.dot(member.astype(BF16), triu_ref[...], preferred_element_type=F32)
    carry = carry_scr[...]
    base = carry[:, 0:1] + prefix
    rank1 = jnp.sum(jnp.where(oh1, base, 0.0), axis=0, keepdims=True)
    rank2 = jnp.sum(jnp.where(oh2, base, 0.0), axis=0, keepdims=True)
    carry_new = carry + jnp.sum(member, axis=1, keepdims=True)
    carry_scr[...] = carry_new
    cnt_ref[...] = carry_new.astype(I32)
    meta_ref[...] = jnp.where(
        sub == 0, i1, jnp.where(
            sub == 1, i2, jnp.where(
                sub == 2, rank1.astype(I32), jnp.where(
                    sub == 3, rank2.astype(I32), 0))))

    lane = lax.broadcasted_iota(I32, lr.shape, 1)
    lr = jnp.where(lane < N_EXPERTS, lr, -jnp.inf)
    m1r = jnp.max(lr, axis=-1, keepdims=True)
    i1r = jnp.min(jnp.where(lr == m1r, lane, LANES), axis=-1, keepdims=True)
    m2r = jnp.max(jnp.where(lane == i1r, -jnp.inf, lr), axis=-1, keepdims=True)
    e2 = jnp.exp(m2r - m1r)
    w1 = 1.0 / (1.0 + e2)
    w2 = e2 * w1
    wts_ref[...] = jnp.where(lane == 0, w1, jnp.where(lane == 1, w2, 0.0))


def _route(x, norm_w, router_w):
    t = x.shape[0]
    rwp = jnp.pad(router_w, ((0, 0), (0, LANES - N_EXPERTS)))
    triu = jnp.triu(jnp.ones((TM_ROUTE, TM_ROUTE), BF16), k=1)
    return pl.pallas_call(
        _route_kernel,
        out_shape=(
            jax.ShapeDtypeStruct((t, D_MODEL), F32),
            jax.ShapeDtypeStruct((SUBLANES, t), I32),
            jax.ShapeDtypeStruct((t, LANES), F32),
            jax.ShapeDtypeStruct((N_EXPERTS, LANES), I32),
        ),
        grid=(t // TM_ROUTE,),
        in_specs=[
            pl.BlockSpec((TM_ROUTE, D_MODEL), lambda i: (i, 0)),
            pl.BlockSpec((1, D_MODEL), lambda i: (0, 0)),
            pl.BlockSpec((D_MODEL, LANES), lambda i: (0, 0)),
            pl.BlockSpec((TM_ROUTE, TM_ROUTE), lambda i: (0, 0)),
        ],
        out_specs=(
            pl.BlockSpec((TM_ROUTE, D_MODEL), lambda i: (i, 0)),
            pl.BlockSpec((SUBLANES, TM_ROUTE), lambda i: (0, i)),
            pl.BlockSpec((TM_ROUTE, LANES), lambda i: (i, 0)),
            pl.BlockSpec((N_EXPERTS, LANES), lambda i: (0, 0)),
        ),
        scratch_shapes=[pltpu.VMEM((N_EXPERTS, LANES), F32)],
        compiler_params=pltpu.CompilerParams(
            dimension_semantics=("arbitrary",),
            vmem_limit_bytes=VMEM_LIMIT),
        name="moe_route",
    )(x, norm_w, rwp, triu)


def _store_rows_as_tiles(ref, value):
    n = value.shape[0]
    for c in range(ROW_TILE):
        ref[pl.ds(c, n, stride=ROW_TILE), :] = value[:, c * LANES:(c + 1) * LANES]


def _load_tile_chunk(ref, c, n):
    return ref[pl.ds(c, n, stride=ROW_TILE), :]


def _pos_copy(pos_hbm, pos_smem, sem, step, slot):
    n = pos_hbm.shape[1]
    dst = pos_smem.at[pl.ds(pl.multiple_of(slot * n, n), n)]
    return pltpu.make_async_copy(pos_hbm.at[step], dst, sem.at[slot])


def _dispatch_kernel(zt_ref, pos_hbm, h_ref, xs_hbm, pos_smem, hbuf, zbuf, psem, dsem, zsem):
    i = pl.program_id(0)
    n = pl.num_programs(0)
    slot = i % 2

    @pl.when(i == 0)
    def _():
        _pos_copy(pos_hbm, pos_smem, psem, 0, 0).start()

    _pos_copy(pos_hbm, pos_smem, psem, i, slot).wait()

    @pl.when(i + 1 < n)
    def _():
        _pos_copy(pos_hbm, pos_smem, psem, i + 1, 1 - slot).start()

    def wait_rows_from(s):
        for _ in range(TOP_K):
            pltpu.make_async_copy(hbuf.at[s], xs_hbm.at[pl.ds(0, TM_DISP * ROW_TILE)],
                                  dsem.at[s]).wait()

    @pl.when(i == 0)
    def _():
        zbuf[...] = jnp.zeros_like(zbuf)
        n_tiles = xs_hbm.shape[0] // (TM_GRP * ROW_TILE)

        def zero_tile_copy(k):
            r0 = pl.multiple_of(k * (TM_GRP * ROW_TILE), TM_GRP * ROW_TILE)
            return pltpu.make_async_copy(zbuf, xs_hbm.at[pl.ds(r0, TM_GRP * ROW_TILE)], zsem)

        def for_each_zero_tile(fn):
            for e in range(N_EXPERTS):
                @pl.when(zt_ref[N_EXPERTS + e] > 0)
                def _():
                    fn(zero_tile_copy(zt_ref[e]))

                @pl.when(zt_ref[2 * N_EXPERTS] + e < n_tiles)
                def _():
                    fn(zero_tile_copy(zt_ref[2 * N_EXPERTS] + e))

        for_each_zero_tile(lambda cp: cp.start())
        for_each_zero_tile(lambda cp: cp.wait())

    @pl.when(i >= 2)
    def _():
        wait_rows_from(slot)

    pos_base = slot * (TOP_K * TM_DISP)
    ring = hbuf.at[slot]

    def move_group(g, carry):
        r0 = pl.multiple_of(g * ROW_GROUP, ROW_GROUP)
        t0 = pl.multiple_of(g * (ROW_GROUP * ROW_TILE), ROW_GROUP * ROW_TILE)
        for c in range(ROW_TILE):
            ring[pl.ds(t0 + c, ROW_GROUP, stride=ROW_TILE), :] = (
                h_ref[pl.ds(r0, ROW_GROUP), c * LANES:(c + 1) * LANES])
        for r in range(ROW_GROUP):
            src = ring.at[pl.ds(pl.multiple_of(t0 + r * ROW_TILE, ROW_TILE), ROW_TILE)]
            for j in range(TOP_K):
                p = pl.multiple_of(pos_smem[pos_base + j * TM_DISP + r0 + r], ROW_TILE)
                pltpu.make_async_copy(src, xs_hbm.at[pl.ds(p, ROW_TILE)],
                                      dsem.at[slot]).start(priority=j)
        return carry

    lax.fori_loop(0, TM_DISP // ROW_GROUP, move_group, 0)

    @pl.when(i == n - 1)
    def _():
        wait_rows_from(slot)

        @pl.when(n > 1)
        def _():
            wait_rows_from(1 - slot)


def _dispatch(h, pos_tiles, zero_tiles, n_sorted_rows):
    t = h.shape[0]
    return pl.pallas_call(
        _dispatch_kernel,
        out_shape=jax.ShapeDtypeStruct((n_sorted_rows * ROW_TILE, LANES), F32),
        grid_spec=pltpu.PrefetchScalarGridSpec(
            num_scalar_prefetch=1,
            grid=(t // TM_DISP,),
            in_specs=[pl.BlockSpec(memory_space=pl.ANY),
                      pl.BlockSpec((TM_DISP, D_MODEL), lambda i, zt: (i, 0))],
            out_specs=pl.BlockSpec(memory_space=pl.ANY),
            scratch_shapes=[
                pltpu.SMEM((2 * TOP_K * TM_DISP,), I32),
                pltpu.VMEM((2, TM_DISP * ROW_TILE, LANES), F32),
                pltpu.VMEM((TM_GRP * ROW_TILE, LANES), F32),
                pltpu.SemaphoreType.DMA((2,)),
                pltpu.SemaphoreType.DMA((2,)),
                pltpu.SemaphoreType.DMA(()),
            ]),
        compiler_params=pltpu.CompilerParams(
            dimension_semantics=("arbitrary",),
            vmem_limit_bytes=VMEM_LIMIT),
        name="moe_dispatch",
    )(zero_tiles, pos_tiles, h)


def _experts_kernel(te_ref, nv_ref, xs_ref, w1_ref, w3_ref, w2_ref, y_ref, xb_scr):
    k = pl.program_id(0)

    @pl.when(k < nv_ref[0])
    def _():
        for c in range(ROW_TILE):
            xb_scr[:, c * LANES:(c + 1) * LANES] = _load_tile_chunk(
                xs_ref, c, TM_GRP).astype(BF16)
        xb = xb_scr[...]
        y = None
        for lo, hi in zip(FF_SPLITS[:-1], FF_SPLITS[1:]):
            cols = slice(lo, hi)
            a = jnp.dot(xb, w1_ref[0, :, cols], preferred_element_type=F32)
            b = jnp.dot(xb, w3_ref[0, :, cols], preferred_element_type=F32)
            act = (_silu(a) * b).astype(BF16)
            part = jnp.dot(act, w2_ref[0, cols, :], preferred_element_type=F32)
            y = part if y is None else y + part
        _store_rows_as_tiles(y_ref, y)

    @pl.when(k >= nv_ref[0])
    def _():
        y_ref[...] = jnp.zeros_like(y_ref)


def _experts(xs, tile_expert, n_valid, w1, w3, w2):
    n_rows = xs.shape[0] // ROW_TILE

    def row_map(k, te, nv):
        return (jnp.minimum(k, nv[0] - 1), 0)

    def out_map(k, te, nv):
        return (k, 0)

    def w_map(k, te, nv):
        return (te[jnp.minimum(k, nv[0] - 1)], 0, 0)

    def weight_spec(shape):
        return pl.BlockSpec((1,) + shape, w_map)

    return pl.pallas_call(
        _experts_kernel,
        out_shape=jax.ShapeDtypeStruct((n_rows * ROW_TILE, LANES), F32),
        grid_spec=pltpu.PrefetchScalarGridSpec(
            num_scalar_prefetch=2,
            grid=(n_rows // TM_GRP,),
            in_specs=[
                pl.BlockSpec((TM_GRP * ROW_TILE, LANES), row_map),
                weight_spec((D_MODEL, D_FF)),
                weight_spec((D_MODEL, D_FF)),
                weight_spec((D_FF, D_MODEL)),
            ],
            out_specs=pl.BlockSpec((TM_GRP * ROW_TILE, LANES), out_map),
            scratch_shapes=[pltpu.VMEM((TM_GRP, D_MODEL), BF16)]),
        compiler_params=pltpu.CompilerParams(
            dimension_semantics=("arbitrary",),
            vmem_limit_bytes=VMEM_LIMIT),
        name="moe_experts",
    )(tile_expert, n_valid, xs, w1, w3, w2)


def _combine_kernel(fuse_norm, pos_hbm, y_hbm, x_ref, wts_ref, fnw_ref, o_ref,
                    pos_smem, ybuf, psem, gsem):
    i = pl.program_id(0)
    n = pl.num_programs(0)
    slot = i % 2

    def issue_gathers(s):
        pos_base = s * (TOP_K * TM_COMB)

        def body(t, carry):
            dst_rows = pl.ds(pl.multiple_of(t * ROW_TILE, ROW_TILE), ROW_TILE)
            for j in range(TOP_K):
                p = pl.multiple_of(pos_smem[pos_base + j * TM_COMB + t], ROW_TILE)
                pltpu.make_async_copy(y_hbm.at[pl.ds(p, ROW_TILE)],
                                      ybuf.at[s, j, dst_rows], gsem.at[s]).start(priority=j)
            return carry

        lax.fori_loop(0, TM_COMB, body, 0, unroll=DMA_ISSUE_UNROLL)

    @pl.when(i == 0)
    def _():
        first = _pos_copy(pos_hbm, pos_smem, psem, 0, 0)
        first.start()
        first.wait()
        issue_gathers(0)

        @pl.when(n > 1)
        def _():
            _pos_copy(pos_hbm, pos_smem, psem, 1, 1).start()

    @pl.when(i + 1 < n)
    def _():
        _pos_copy(pos_hbm, pos_smem, psem, i + 1, 1 - slot).wait()
        issue_gathers(1 - slot)

        @pl.when(i + 2 < n)
        def _():
            _pos_copy(pos_hbm, pos_smem, psem, i + 2, slot).start()

    for j in range(TOP_K):
        pltpu.make_async_copy(y_hbm.at[pl.ds(0, TM_COMB * ROW_TILE)], ybuf.at[slot, j],
                              gsem.at[slot]).wait()

    w = wts_ref[...]
    w1 = w[:, 0:1]
    w2 = w[:, 1:2]
    chunks = []
    for c in range(ROW_TILE):
        chunks.append(x_ref[:, c * LANES:(c + 1) * LANES]
                      + w1 * _load_tile_chunk(ybuf.at[slot, 0], c, TM_COMB)
                      + w2 * _load_tile_chunk(ybuf.at[slot, 1], c, TM_COMB))
    if fuse_norm:
        ss = chunks[0] * chunks[0]
        for ch in chunks[1:]:
            ss = ss + ch * ch
        inv = lax.rsqrt(jnp.sum(ss, axis=-1, keepdims=True) * (1.0 / D_MODEL) + EPS)
        fnw = fnw_ref[...]
        chunks = [ch * inv * fnw[:, c * LANES:(c + 1) * LANES] for c, ch in enumerate(chunks)]
    for c, ch in enumerate(chunks):
        o_ref[:, c * LANES:(c + 1) * LANES] = ch


def _combine(x, y, pos_tiles, wts, final_norm_w, fuse_norm):
    t = x.shape[0]
    return pl.pallas_call(
        functools.partial(_combine_kernel, fuse_norm),
        out_shape=jax.ShapeDtypeStruct((t, D_MODEL), F32),
        grid=(t // TM_COMB,),
        in_specs=[
            pl.BlockSpec(memory_space=pl.ANY),
            pl.BlockSpec(memory_space=pl.ANY),
            pl.BlockSpec((TM_COMB, D_MODEL), lambda i: (i, 0)),
            pl.BlockSpec((TM_COMB, LANES), lambda i: (i, 0)),
            pl.BlockSpec((1, D_MODEL), lambda i: (0, 0)),
        ],
        out_specs=pl.BlockSpec((TM_COMB, D_MODEL), lambda i: (i, 0)),
        scratch_shapes=[
            pltpu.SMEM((2 * TOP_K * TM_COMB,), I32),
            pltpu.VMEM((2, TOP_K, TM_COMB * ROW_TILE, LANES), F32),
            pltpu.SemaphoreType.DMA((2,)),
            pltpu.SemaphoreType.DMA((2,)),
        ],
        compiler_params=pltpu.CompilerParams(dimension_semantics=("arbitrary",)),
        name="moe_combine",
    )(pos_tiles, y, x, wts, final_norm_w)


def _tile_major(pos, tile):
    t = pos.shape[1]
    return pos.reshape(TOP_K, t // tile, tile).transpose(1, 0, 2).reshape(t // tile, TOP_K * tile)


def _moe_ffn(x, norm_w, router_w, w1, w3, w2, final_norm_w, fuse_norm):
    t = x.shape[0]
    h, meta, wts, cnt = _route(x, norm_w, router_w)

    counts = cnt[:, 0]
    padded = ((counts + TM_GRP - 1) // TM_GRP) * TM_GRP
    ends = jnp.cumsum(padded)
    offs = ends - padded
    n_sorted_rows = TOP_K * t + N_EXPERTS * TM_GRP
    tile_start = jnp.arange(n_sorted_rows // TM_GRP, dtype=I32) * TM_GRP
    tile_expert = jnp.minimum(
        jnp.sum((tile_start[:, None] >= ends[None, :]).astype(I32), axis=1), N_EXPERTS - 1)
    n_valid = (ends[-1] // TM_GRP).astype(I32).reshape(1)
    pos = (jnp.stack([offs[meta[0]] + meta[2], offs[meta[1]] + meta[3]]) * ROW_TILE).astype(I32)
    zero_tiles = jnp.concatenate([ends // TM_GRP - 1, padded, n_valid]).astype(I32)

    xs = _dispatch(h, _tile_major(pos, TM_DISP), zero_tiles, n_sorted_rows)
    y = _experts(xs, tile_expert.astype(I32), n_valid, w1, w3, w2)
    return _combine(x, y, _tile_major(pos, TM_COMB), wts, final_norm_w, fuse_norm)


def _final_norm_kernel(x_ref, nw_ref, o_ref):
    o_ref[...] = _rms_norm_rows(x_ref[...], nw_ref[...])


def _final_norm(x, norm_w):
    t = x.shape[0]
    return pl.pallas_call(
        _final_norm_kernel,
        out_shape=jax.ShapeDtypeStruct((t, D_MODEL), F32),
        grid=(t // TM_PROJ,),
        in_specs=[
            pl.BlockSpec((TM_PROJ, D_MODEL), lambda i: (i, 0)),
            pl.BlockSpec((1, D_MODEL), lambda i: (0, 0)),
        ],
        out_specs=pl.BlockSpec((TM_PROJ, D_MODEL), lambda i: (i, 0)),
        compiler_params=pltpu.CompilerParams(dimension_semantics=("arbitrary",)),
        name="final_norm",
    )(x, norm_w)


def kernel(x, w_in, lower_bounds, hgrn_norm_w, conv_w, w_proj_hgrn, w_proj_conv, w_out,
           norm_mix, norm_ffn, dense_w1, dense_w3, dense_w2, router_w,
           expert_w1, expert_w3, expert_w2, final_norm):
    batch, seq, d = x.shape
    depth = w_in.shape[0]
    assert d == D_MODEL and seq % CT_REC == 0 and (batch * seq) % TM_PROJ == 0
    assert w_in.shape[2] == N_GROUPS * D_MODEL

    xt = x.reshape(batch * seq, d)
    tri = jnp.tril(jnp.ones((CHUNK, CHUNK), F32))
    lower_bounds = lower_bounds.astype(F32)
    final_w = final_norm.reshape(1, d)
    normed = False

    for layer in range(depth):
        proj = _inproj(xt, norm_mix[layer].reshape(1, d), w_in[layer].astype(BF16))
        o_gated = _recurrence(proj, lower_bounds, hgrn_norm_w[layer].reshape(1, HEAD_DIM),
                              tri, layer, batch, seq)
        xt = _mixout(xt, o_gated, proj, conv_w[layer],
                     w_proj_hgrn[layer].astype(BF16), w_proj_conv[layer].astype(BF16),
                     w_out[layer].astype(BF16), seq)
        j = layer // 2
        nw = norm_ffn[layer].reshape(1, d)
        if layer % 2 == 0:
            xt = _dense_ffn(xt, nw, dense_w1[j].astype(BF16), dense_w3[j].astype(BF16),
                            dense_w2[j].astype(BF16))
        else:
            normed = layer == depth - 1
            xt = _moe_ffn(xt, nw, router_w[j].astype(F32), expert_w1[j].astype(BF16),
                          expert_w3[j].astype(BF16), expert_w2[j].astype(BF16),
                          final_w, normed)
    out = xt if normed else _final_norm(xt, final_w)
    return out.reshape(batch, seq, d)
```
